```python
import jax, jax.numpy as jnp
from jax import lax
import numpy as np

D_MODEL = 2048
BATCH = 4
SEQ = 2048
DEPTH = 1
DEC_BATCH = 128
DEC_SEQ = 1
PAST_LEN = 16384
PAGE_SIZE = 128

D_MIX = D_MODEL
D_CONF = D_MIX // 2
D_LRU = D_MIX - D_CONF
N_CONF_GROUPS = 8
N_LRU_HEADS = 8
LRU_HEAD_DIM = D_LRU // N_LRU_HEADS
CONF_KERNEL = 31
LRU_CONV = 4
LRU_C = 8.0
N_GROUPS = 4
EXPERTS_PER_GROUP = 8
N_EXPERTS = N_GROUPS * EXPERTS_PER_GROUP
TOP_K = 2
D_EXPERT = D_MODEL // 4
D_PLE = 256
EPS = 1e-6
D_IN = 2 * D_CONF + 2 * D_LRU

kernel_name = "hybrid_conformer_rglru_hmoe_step"


def rmsnorm(x, g):
    xf = x.astype(jnp.float32)
    y = xf * lax.rsqrt(jnp.mean(xf * xf, axis=-1, keepdims=True) + EPS) * g.astype(jnp.float32)
    return y.astype(x.dtype)


def layernorm(x, g, b):
    xf = x.astype(jnp.float32)
    mu = jnp.mean(xf, axis=-1, keepdims=True)
    var = jnp.mean(jnp.square(xf - mu), axis=-1, keepdims=True)
    y = (xf - mu) * lax.rsqrt(var + EPS) * g.astype(jnp.float32) + b.astype(jnp.float32)
    return y.astype(x.dtype)


def causal_dw_conv(u, buf, w, b):
    K, C = w.shape
    full = jnp.concatenate([buf.astype(u.dtype), u], axis=1)
    out = lax.conv_general_dilated(full, w.astype(u.dtype)[:, None, :], window_strides=(1,), padding='VALID',
                                   dimension_numbers=('NWC', 'WIO', 'NWC'), feature_group_count=C)
    return out + b.astype(u.dtype), full[:, full.shape[1] - (K - 1):]


def rg_lru(x, h0, w_a, b_a, w_x, b_x, lam):
    B, T, C = x.shape
    xh = x.reshape(B, T, N_LRU_HEADS, LRU_HEAD_DIM)
    r = jax.nn.sigmoid((jnp.einsum('bthi,hij->bthj', xh, w_a) + b_a).astype(jnp.float32)).reshape(B, T, C)
    i = jax.nn.sigmoid((jnp.einsum('bthi,hij->bthj', xh, w_x) + b_x).astype(jnp.float32)).reshape(B, T, C)
    log_a = -LRU_C * r * jax.nn.softplus(-lam.astype(jnp.float32))
    a = jnp.exp(log_a)
    u = jnp.sqrt(-jnp.expm1(2.0 * log_a)) * i * x.astype(jnp.float32)

    def step(h, inp):
        a_t, u_t = inp
        h = a_t * h + u_t
        return h, h

    h_last, hs = lax.scan(step, h0.astype(jnp.float32), (jnp.swapaxes(a, 0, 1), jnp.swapaxes(u, 0, 1)))
    return jnp.swapaxes(hs, 0, 1).astype(x.dtype), h_last.astype(h0.dtype)


def hier_moe(xn, w_grp, b_grp, w_rt, b_rt, w_gate, w_up, w_down):
    B, T, D = xn.shape
    xt = xn.reshape(-1, D)
    g_logits = (xt @ w_grp + b_grp).astype(jnp.float32)
    g_prob = jax.nn.softmax(g_logits, axis=-1)
    g_idx = jnp.argmax(g_logits, axis=-1)
    g_w = jnp.take_along_axis(g_prob, g_idx[:, None], axis=-1)
    e_logits = (xt @ w_rt + b_rt).astype(jnp.float32).reshape(-1, N_GROUPS, EXPERTS_PER_GROUP)
    e_in = jnp.take_along_axis(e_logits, g_idx[:, None, None], axis=1)[:, 0]
    top_v, top_i = lax.top_k(e_in, TOP_K)
    weights = g_w * jax.nn.softmax(top_v, axis=-1)
    expert_id = g_idx[:, None] * EXPERTS_PER_GROUP + top_i
    combine = jnp.einsum('nk,nke->ne', weights,
                         jax.nn.one_hot(expert_id, N_EXPERTS, dtype=jnp.float32)).astype(xt.dtype)
    out = jnp.zeros_like(xt)
    for grp in range(N_GROUPS):
        sl = slice(grp * EXPERTS_PER_GROUP, (grp + 1) * EXPERTS_PER_GROUP)
        hid = jax.nn.silu(jnp.einsum('nd,edf->nef', xt, w_gate[sl])) * jnp.einsum('nd,edf->nef', xt, w_up[sl])
        out = out + jnp.einsum('nef,efd->nd', hid * combine[:, sl, None], w_down[sl])
    return out.reshape(B, T, D)


def run_trunk(x, p, conf_bufs, lru_bufs, lru_hs,
              norm1_g, w_in, b_in, conf_dw_w, conf_dw_b, conf_ln_g, conf_ln_b,
              lru_conv_w, lru_conv_b, lru_wa, lru_ba, lru_wx, lru_bx, lru_lambda, w_out,
              norm2_g, w_grp, b_grp, w_rt, b_rt, w_gate, w_up, w_down,
              ple_norm_g, w_ple, w_pg, b_pg, final_g):
    h = x
    new_conf, new_lconv, new_h = [], [], []
    for l in range(DEPTH):
        xn = rmsnorm(h, norm1_g[l])
        proj = xn @ w_in[l] + b_in[l]
        c_val, c_gate, l_x, l_gate = jnp.split(proj, [D_CONF, 2 * D_CONF, 2 * D_CONF + D_LRU], axis=-1)
        u = c_val * jax.nn.sigmoid(c_gate)
        c, cb = causal_dw_conv(u, conf_bufs[l], conf_dw_w[l], conf_dw_b[l])
        c = jax.nn.silu(layernorm(c, conf_ln_g[l], conf_ln_b[l]))
        lx, lb = causal_dw_conv(l_x, lru_bufs[l], lru_conv_w[l], lru_conv_b[l])
        hs, hl = rg_lru(lx, lru_hs[l], lru_wa[l], lru_ba[l], lru_wx[l], lru_bx[l], lru_lambda[l])
        r_out = hs * jax.nn.gelu(l_gate)
        h = h + jnp.concatenate([c, r_out], axis=-1) @ w_out[l]
        h = h + hier_moe(rmsnorm(h, norm2_g[l]), w_grp[l], b_grp[l], w_rt[l], b_rt[l],
                         w_gate[l], w_up[l], w_down[l])
        gate = jax.nn.sigmoid(rmsnorm(h, ple_norm_g[l]) @ w_pg[l] + b_pg[l])
        h = h + gate * (p[l] @ w_ple[l])
        new_conf.append(cb); new_lconv.append(lb); new_h.append(hl)
    return rmsnorm(h, final_g), jnp.stack(new_conf), jnp.stack(new_lconv), jnp.stack(new_h)


def setup_inputs(seed: int = 0) -> dict:
    key = jax.random.key(seed)
    ks = iter(jax.random.split(key, 64))
    f32 = jnp.float32

    def nrm(shape, s):
        return jax.random.normal(next(ks), shape, f32) * s

    a0 = jax.random.uniform(next(ks), (DEPTH, D_LRU), f32, minval=0.9, maxval=0.999)
    return {
        "x_prompt": nrm((BATCH, SEQ, D_MODEL), 1.0),
        "x_sample": nrm((DEC_BATCH, DEC_SEQ, D_MODEL), 1.0),
        "state_conf_conv": nrm((DEPTH, DEC_BATCH, CONF_KERNEL - 1, D_CONF), 0.5),
        "state_lru_conv": nrm((DEPTH, DEC_BATCH, LRU_CONV - 1, D_LRU), 1.0),
        "state_lru_h": nrm((DEPTH, DEC_BATCH, D_LRU), 0.5),
        "p_prompt": nrm((DEPTH, BATCH, SEQ, D_PLE), 1.0),
        "p_sample": nrm((DEPTH, DEC_BATCH, DEC_SEQ, D_PLE), 1.0),
        "norm1_g": 1.0 + nrm((DEPTH, D_MODEL), 0.01),
        "w_in": nrm((DEPTH, D_MODEL, D_IN), D_MODEL ** -0.5),
        "b_in": nrm((DEPTH, D_IN), 0.01),
        "conf_dw_w": nrm((DEPTH, CONF_KERNEL, D_CONF), CONF_KERNEL ** -0.5),
        "conf_dw_b": nrm((DEPTH, D_CONF), 0.01),
        "conf_ln_g": 1.0 + nrm((DEPTH, D_CONF), 0.01),
        "conf_ln_b": nrm((DEPTH, D_CONF), 0.01),
        "lru_conv_w": nrm((DEPTH, LRU_CONV, D_LRU), LRU_CONV ** -0.5),
        "lru_conv_b": nrm((DEPTH, D_LRU), 0.01),
        "lru_wa": nrm((DEPTH, N_LRU_HEADS, LRU_HEAD_DIM, LRU_HEAD_DIM), LRU_HEAD_DIM ** -0.5),
        "lru_ba": nrm((DEPTH, N_LRU_HEADS, LRU_HEAD_DIM), 0.01),
        "lru_wx": nrm((DEPTH, N_LRU_HEADS, LRU_HEAD_DIM, LRU_HEAD_DIM), LRU_HEAD_DIM ** -0.5),
        "lru_bx": nrm((DEPTH, N_LRU_HEADS, LRU_HEAD_DIM), 0.01),
        "lru_lambda": jnp.log(a0) - jnp.log1p(-a0),
        "w_out": nrm((DEPTH, D_MIX, D_MODEL), D_MIX ** -0.5),
        "norm2_g": 1.0 + nrm((DEPTH, D_MODEL), 0.01),
        "w_grp": nrm((DEPTH, D_MODEL, N_GROUPS), D_MODEL ** -0.5),
        "b_grp": nrm((DEPTH, N_GROUPS), 0.01),
        "w_rt": nrm((DEPTH, D_MODEL, N_EXPERTS), D_MODEL ** -0.5),
        "b_rt": nrm((DEPTH, N_EXPERTS), 0.01),
        "w_gate": nrm((DEPTH, N_EXPERTS, D_MODEL, D_EXPERT), D_MODEL ** -0.5),
        "w_up": nrm((DEPTH, N_EXPERTS, D_MODEL, D_EXPERT), D_MODEL ** -0.5),
        "w_down": nrm((DEPTH, N_EXPERTS, D_EXPERT, D_MODEL), D_EXPERT ** -0.5),
        "ple_norm_g": 1.0 + nrm((DEPTH, D_MODEL), 0.01),
        "w_ple": nrm((DEPTH, D_PLE, D_MODEL), D_PLE ** -0.5),
        "w_pg": nrm((DEPTH, D_MODEL, D_MODEL), D_MODEL ** -0.5),
        "b_pg": nrm((DEPTH, D_MODEL), 0.01),
        "final_g": 1.0 + nrm((D_MODEL,), 0.01),
    }


def reference(x_prompt, x_sample, state_conf_conv, state_lru_conv, state_lru_h, p_prompt, p_sample,
              norm1_g, w_in, b_in, conf_dw_w, conf_dw_b, conf_ln_g, conf_ln_b,
              lru_conv_w, lru_conv_b, lru_wa, lru_ba, lru_wx, lru_bx, lru_lambda, w_out,
              norm2_g, w_grp, b_grp, w_rt, b_rt, w_gate, w_up, w_down,
              ple_norm_g, w_ple, w_pg, b_pg, final_g):
    weights = (norm1_g, w_in, b_in, conf_dw_w, conf_dw_b, conf_ln_g, conf_ln_b,
               lru_conv_w, lru_conv_b, lru_wa, lru_ba, lru_wx, lru_bx, lru_lambda, w_out,
               norm2_g, w_grp, b_grp, w_rt, b_rt, w_gate, w_up, w_down,
               ple_norm_g, w_ple, w_pg, b_pg, final_g)
    B = x_prompt.shape[0]
    zc = jnp.zeros((DEPTH, B, CONF_KERNEL - 1, D_CONF), x_prompt.dtype)
    zl = jnp.zeros((DEPTH, B, LRU_CONV - 1, D_LRU), x_prompt.dtype)
    zh = jnp.zeros((DEPTH, B, D_LRU), state_lru_h.dtype)
    y_prompt, conf_conv_prompt, lru_conv_prompt, lru_h_prompt = run_trunk(
        x_prompt, p_prompt, zc, zl, zh, *weights)
    y_sample, conf_conv_sample, lru_conv_sample, lru_h_sample = run_trunk(
        x_sample, p_sample, state_conf_conv, state_lru_conv, state_lru_h, *weights)
    return (y_prompt, y_sample, conf_conv_prompt, conf_conv_sample,
            lru_conv_prompt, lru_conv_sample, lru_h_prompt, lru_h_sample)
```

```python
import functools

import jax
import jax.numpy as jnp
from jax import lax
from jax.experimental import pallas as pl
from jax.experimental.pallas import tpu as pltpu

D_MODEL = 2048
D_CONF = 1024
D_LRU = 1024
N_HEADS = 8
HEAD = 128
CONF_K = 31
LRU_K = 4
LRU_C = 8.0
N_GROUPS = 4
EPG = 8
N_EXPERTS = 32
D_EXPERT = 512
D_PLE = 256
EPS = 1e-6

LANES = 128
SUBLANES = 8
VMEM_LIMIT = 56 * 1024 * 1024

TM = 512
TM_E = 256
CONF_TC = 64
LRU_TC = 256
LRU_PITCH = LRU_TC + SUBLANES
GATHER_CH = 1024

_BF16 = jnp.bfloat16
_F32 = jnp.float32


def _params(*sem):
    return pltpu.CompilerParams(dimension_semantics=sem, vmem_limit_bytes=VMEM_LIMIT)


def _const_spec(shape):
    nd = len(shape)
    return pl.BlockSpec(shape, lambda *_: (0,) * nd, pipeline_mode=pl.Buffered(1))


def _rms(x, g):
    return x * lax.rsqrt(jnp.mean(x * x, axis=-1, keepdims=True) + EPS) * g


def _dot(a, b):
    return jnp.dot(a, b, preferred_element_type=_F32)


def _inproj_body(x_ref, g_ref, w_ref, b_ref, u_ref, lx_ref, lg_ref):
    xn = _rms(x_ref[...], g_ref[...]).astype(_BF16)

    def proj(k):
        sl = slice(k * D_CONF, (k + 1) * D_CONF)
        return _dot(xn, w_ref[:, sl]) + b_ref[:, sl]

    u_ref[...] = proj(0) * jax.nn.sigmoid(proj(1))
    lx_ref[...] = proj(2)
    lg_ref[...] = jax.nn.gelu(proj(3))


def _inproj(x, g, w, b, tm):
    n = x.shape[0]
    out = jax.ShapeDtypeStruct((n, D_CONF), _F32)
    tok = lambda width: pl.BlockSpec((tm, width), lambda i: (i, 0))
    return pl.pallas_call(
        _inproj_body,
        out_shape=(out, out, out),
        grid=(n // tm,),
        in_specs=[tok(D_MODEL), _const_spec((1, D_MODEL)), _const_spec((D_MODEL, 4 * D_CONF)),
                  _const_spec((1, 4 * D_CONF))],
        out_specs=(tok(D_CONF), tok(D_CONF), tok(D_CONF)),
        compiler_params=_params("arbitrary"),
        name="inproj",
    )(x, g, w, b)


def _layernorm_silu(conv, g, b):
    mu = jnp.mean(conv, axis=-1, keepdims=True)
    cen = conv - mu
    var = jnp.mean(cen * cen, axis=-1, keepdims=True)
    y = cen * lax.rsqrt(var + EPS) * g + b
    return y * jax.nn.sigmoid(y)


def _conf_body(u_ref, w_ref, b_ref, g_ref, lb_ref, c_ref, s_ref, *, seq):
    head = 4 * SUBLANES
    s_ref[0:head, :] = jnp.zeros((head, D_CONF), _F32)
    s_ref[head:head + seq, :] = u_ref[...]
    s_ref[head + seq:, :] = jnp.zeros((SUBLANES, D_CONF), _F32)
    tc = CONF_TC

    def chunk(ci, carry):
        t0 = pl.multiple_of(ci * tc, tc)
        win = s_ref[pl.ds(t0, tc + 5 * SUBLANES), :]
        out = None
        for r in range(SUBLANES):
            part = None
            for q in range(5):
                m = SUBLANES * q + r
                if 2 <= m <= CONF_K + 1:
                    term = win[SUBLANES * q:SUBLANES * q + tc + SUBLANES, :] * w_ref[m - 2:m - 1, :]
                    part = term if part is None else part + term
            shifted = part[r:r + tc, :]
            out = shifted if out is None else out + shifted
        c = _layernorm_silu(out + b_ref[...], g_ref[...], lb_ref[...])
        c_ref[pl.ds(t0, tc), :] = c.astype(_BF16)
        return carry

    lax.fori_loop(0, seq // tc, chunk, 0)


def _conf_prompt(u, w, b, g, lb, batch, seq):
    return pl.pallas_call(
        functools.partial(_conf_body, seq=seq),
        out_shape=jax.ShapeDtypeStruct((batch * seq, D_CONF), _BF16),
        grid=(batch,),
        in_specs=[pl.BlockSpec((seq, D_CONF), lambda i: (i, 0)), _const_spec((CONF_K, D_CONF)),
                  _const_spec((1, D_CONF)), _const_spec((1, D_CONF)), _const_spec((1, D_CONF))],
        out_specs=pl.BlockSpec((seq, D_CONF), lambda i: (i, 0)),
        scratch_shapes=[pltpu.VMEM((seq + 5 * SUBLANES, D_CONF), _F32)],
        compiler_params=_params("arbitrary"),
        name="conf_prompt",
    )(u, w, b, g, lb)


def _conf_step_body(buf_ref, u_ref, w_ref, b_ref, g_ref, lb_ref, c_ref):
    w = w_ref[...]
    conv = jnp.sum(buf_ref[...] * w[None, :CONF_K - 1, :], axis=1) + u_ref[...] * w[CONF_K - 1:CONF_K, :]
    c_ref[...] = _layernorm_silu(conv + b_ref[...], g_ref[...], lb_ref[...]).astype(_BF16)


def _conf_step(buf, u, w, b, g, lb):
    nb = buf.shape[0]
    bb = 16
    return pl.pallas_call(
        _conf_step_body,
        out_shape=jax.ShapeDtypeStruct((nb, D_CONF), _BF16),
        grid=(nb // bb,),
        in_specs=[pl.BlockSpec((bb, CONF_K - 1, D_CONF), lambda i: (i, 0, 0)),
                  pl.BlockSpec((bb, D_CONF), lambda i: (i, 0)), _const_spec((CONF_K, D_CONF)),
                  _const_spec((1, D_CONF)), _const_spec((1, D_CONF)), _const_spec((1, D_CONF))],
        out_specs=pl.BlockSpec((bb, D_CONF), lambda i: (i, 0)),
        compiler_params=_params("arbitrary"),
        name="conf_step",
    )(buf, u, w, b, g, lb)


def _lru_gates(lx, wg_ref, bg_ref, lam_ref, head):
    sl = slice(head * HEAD, (head + 1) * HEAD)
    lxh = lx[:, sl]
    z = _dot(lxh.astype(_BF16), wg_ref[head]) + bg_ref[head:head + 1, :]
    r = jax.nn.sigmoid(z[:, :HEAD])
    i = jax.nn.sigmoid(z[:, HEAD:])
    log_a = -LRU_C * r * jax.nn.softplus(-lam_ref[:, sl])
    a = jnp.exp(log_a)
    return a, jnp.sqrt(-jnp.tanh(log_a) * (a * a + 1.0)) * i * lxh


def _lru_body(lx_ref, lg_ref, cw_ref, cb_ref, wg_ref, bg_ref, lam_ref, r_ref, hl_ref,
              halo_ref, h_ref, a_ref, u_ref, hs_ref):
    j = pl.program_id(1)
    tc = LRU_TC

    @pl.when(j == 0)
    def _():
        halo_ref[...] = jnp.zeros_like(halo_ref)
        h_ref[...] = jnp.zeros_like(h_ref)

    cur = lx_ref[...]
    ext = jnp.concatenate([halo_ref[...], cur], axis=0)
    lx = cb_ref[...] + cw_ref[LRU_K - 1:LRU_K, :] * cur
    for back in range(1, LRU_K):
        lx = lx + cw_ref[LRU_K - 1 - back:LRU_K - back, :] * ext[SUBLANES - back:SUBLANES - back + tc, :]
    halo_ref[...] = cur[tc - SUBLANES:, :]

    for head in range(N_HEADS):
        a, u = _lru_gates(lx, wg_ref, bg_ref, lam_ref, head)
        a_ref[head * LRU_PITCH:head * LRU_PITCH + tc, :] = a
        u_ref[head * LRU_PITCH:head * LRU_PITCH + tc, :] = u

    def step(t, h):
        rows = pl.ds(t, N_HEADS, stride=LRU_PITCH)
        h = a_ref[rows, :] * h + u_ref[rows, :]
        hs_ref[rows, :] = h
        return h

    h_ref[...] = lax.fori_loop(0, tc, step, h_ref[...], unroll=8)

    for head in range(N_HEADS):
        sl = slice(head * HEAD, (head + 1) * HEAD)
        hs = hs_ref[head * LRU_PITCH:head * LRU_PITCH + tc, :]
        r_ref[:, sl] = (hs * lg_ref[:, sl]).astype(_BF16)

    @pl.when(j == pl.num_programs(1) - 1)
    def _():
        hl_ref[0] = h_ref[...]


def _lru_prompt(lx, lg, cw, cb, wg, bg, lam, batch, seq):
    nchunk = seq // LRU_TC
    tok = pl.BlockSpec((LRU_TC, D_LRU), lambda b, j: (b * nchunk + j, 0))
    pitch_rows = N_HEADS * LRU_PITCH
    return pl.pallas_call(
        _lru_body,
        out_shape=(jax.ShapeDtypeStruct((batch * seq, D_LRU), _BF16),
                   jax.ShapeDtypeStruct((batch, N_HEADS, HEAD), _F32)),
        grid=(batch, nchunk),
        in_specs=[tok, tok, _const_spec((LRU_K, D_LRU)), _const_spec((1, D_LRU)),
                  _const_spec((N_HEADS, HEAD, 2 * HEAD)), _const_spec((N_HEADS, 2 * HEAD)),
                  _const_spec((1, D_LRU))],
        out_specs=(tok, pl.BlockSpec((1, N_HEADS, HEAD), lambda b, j: (b, 0, 0))),
        scratch_shapes=[pltpu.VMEM((SUBLANES, D_LRU), _F32), pltpu.VMEM((N_HEADS, HEAD), _F32),
                        pltpu.VMEM((pitch_rows, HEAD), _F32), pltpu.VMEM((pitch_rows, HEAD), _F32),
                        pltpu.VMEM((pitch_rows, HEAD), _F32)],
        compiler_params=_params("arbitrary", "arbitrary"),
        name="lru_prompt",
    )(lx, lg, cw, cb, wg, bg, lam)


def _lru_step_body(b0_ref, b1_ref, b2_ref, lx_ref, h0_ref, lg_ref, cw_ref, cb_ref, wg_ref, bg_ref, lam_ref,
                   r_ref, hn_ref):
    lx = (cb_ref[...] + cw_ref[0:1, :] * b0_ref[...] + cw_ref[1:2, :] * b1_ref[...]
          + cw_ref[2:3, :] * b2_ref[...] + cw_ref[3:4, :] * lx_ref[...])
    for head in range(N_HEADS):
        sl = slice(head * HEAD, (head + 1) * HEAD)
        a, u = _lru_gates(lx, wg_ref, bg_ref, lam_ref, head)
        h = a * h0_ref[:, sl] + u
        hn_ref[:, sl] = h
        r_ref[:, sl] = (h * lg_ref[:, sl]).astype(_BF16)


def _lru_step(b0, b1, b2, lx, h0, lg, cw, cb, wg, bg, lam):
    nb = lx.shape[0]
    full = _const_spec((nb, D_LRU))
    return pl.pallas_call(
        _lru_step_body,
        out_shape=(jax.ShapeDtypeStruct((nb, D_LRU), _BF16), jax.ShapeDtypeStruct((nb, D_LRU), _F32)),
        grid=(1,),
        in_specs=[full] * 6 + [_const_spec((LRU_K, D_LRU)), _const_spec((1, D_LRU)),
                               _const_spec((N_HEADS, HEAD, 2 * HEAD)), _const_spec((N_HEADS, 2 * HEAD)),
                               _const_spec((1, D_LRU))],
        out_specs=(pl.BlockSpec((nb, D_LRU), lambda i: (0, 0)), pl.BlockSpec((nb, D_LRU), lambda i: (0, 0))),
        compiler_params=_params("arbitrary"),
        name="lru_step",
    )(b0, b1, b2, lx, h0, lg, cw, cb, wg, bg, lam)


def _outproj_body(cp_ref, rp_ref, xp_ref, cs_ref, rs_ref, xs_ref, wc_ref, wr_ref, g_ref, wrt_ref, brt_ref,
                  h1_ref, xn_ref, lg_ref, *, n_ptiles, nb):
    i = pl.program_id(0)

    def mix(c, r, x):
        h1 = x + _dot(c, wc_ref[...]) + _dot(r, wr_ref[...])
        xn = _rms(h1, g_ref[...])
        return h1, xn, _dot(xn.astype(_BF16), wrt_ref[...]) + brt_ref[...]

    @pl.when(i < n_ptiles)
    def _():
        h1_ref[...], xn_ref[...], lg_ref[...] = mix(cp_ref[...], rp_ref[...], xp_ref[...])

    @pl.when(i == n_ptiles)
    def _():
        for ref, val in zip((h1_ref, xn_ref, lg_ref), mix(cs_ref[...], rs_ref[...], xs_ref[...])):
            ref[:nb, :] = val
            ref[nb:, :] = jnp.zeros((ref.shape[0] - nb, ref.shape[1]), _F32)


def _outproj(cp, rp, xp, cs, rs, xs, wc, wr, g, wrt, brt, tm):
    n_ptiles = xp.shape[0] // tm
    nb = xs.shape[0]
    rows = (n_ptiles + 1) * tm
    last = n_ptiles - 1
    tok = lambda width: pl.BlockSpec((tm, width), lambda i: (jnp.minimum(i, last), 0))
    out_tok = lambda width: pl.BlockSpec((tm, width), lambda i: (i, 0))
    return pl.pallas_call(
        functools.partial(_outproj_body, n_ptiles=n_ptiles, nb=nb),
        out_shape=(jax.ShapeDtypeStruct((rows, D_MODEL), _F32), jax.ShapeDtypeStruct((rows, D_MODEL), _F32),
                   jax.ShapeDtypeStruct((rows, LANES), _F32)),
        grid=(n_ptiles + 1,),
        in_specs=[tok(D_CONF), tok(D_LRU), tok(D_MODEL), _const_spec((nb, D_CONF)), _const_spec((nb, D_LRU)),
                  _const_spec((nb, D_MODEL)), _const_spec((D_CONF, D_MODEL)), _const_spec((D_LRU, D_MODEL)),
                  _const_spec((1, D_MODEL)), _const_spec((D_MODEL, LANES)), _const_spec((1, LANES))],
        out_specs=(out_tok(D_MODEL), out_tok(D_MODEL), out_tok(LANES)),
        compiler_params=_params("arbitrary"),
        name="outproj",
    )(cp, rp, xp, cs, rs, xs, wc, wr, g, wrt, brt)


def _gather_body(idx_ref, src_ref, out_ref, sem):
    base = pl.program_id(0) * GATHER_CH

    def copy(i):
        return pltpu.make_async_copy(src_ref.at[pl.ds(idx_ref[0, 0, i], 1), :],
                                     out_ref.at[pl.ds(base + i, 1), :], sem)

    def start(i, carry):
        copy(i).start()
        return carry

    def wait(i, carry):
        copy(i).wait()
        return carry

    lax.fori_loop(0, GATHER_CH, start, 0)
    lax.fori_loop(0, GATHER_CH, wait, 0)


def _gather_rows(src, idx):
    n = idx.shape[0]
    return pl.pallas_call(
        _gather_body,
        out_shape=jax.ShapeDtypeStruct((n, src.shape[1]), src.dtype),
        grid=(n // GATHER_CH,),
        in_specs=[pl.BlockSpec((1, 1, GATHER_CH), lambda j: (j, 0, 0), memory_space=pltpu.SMEM),
                  pl.BlockSpec(memory_space=pl.ANY)],
        out_specs=pl.BlockSpec(memory_space=pl.ANY),
        scratch_shapes=[pltpu.SemaphoreType.DMA(())],
        compiler_params=_params("arbitrary"),
        name="gather_rows",
    )(idx.reshape(n // GATHER_CH, 1, GATHER_CH), src)


def _moe_body(te_ref, nu_ref, x_ref, cw_ref, wg_ref, wu_ref, wd_ref, y_ref):
    j = pl.program_id(0)

    @pl.when(j < nu_ref[0])
    def _():
        x = x_ref[...].astype(_BF16)
        g = _dot(x, wg_ref[0].astype(_BF16))
        u = _dot(x, wu_ref[0].astype(_BF16))
        hid = g * jax.nn.sigmoid(g) * u * cw_ref[...]
        y_ref[...] = _dot(hid.astype(_BF16), wd_ref[0].astype(_BF16))

    @pl.when(j >= nu_ref[0])
    def _():
        y_ref[...] = jnp.zeros_like(y_ref)


def _moe(tile_expert, n_used, xs, cw, w_gate, w_up, w_down):
    n_tiles = tile_expert.shape[0]
    grid_spec = pltpu.PrefetchScalarGridSpec(
        num_scalar_prefetch=2,
        grid=(n_tiles,),
        in_specs=[pl.BlockSpec((TM_E, D_MODEL), lambda j, te, nu: (j, 0)),
                  pl.BlockSpec((TM_E, 1), lambda j, te, nu: (j, 0)),
                  pl.BlockSpec((1, D_MODEL, D_EXPERT), lambda j, te, nu: (te[j], 0, 0)),
                  pl.BlockSpec((1, D_MODEL, D_EXPERT), lambda j, te, nu: (te[j], 0, 0)),
                  pl.BlockSpec((1, D_EXPERT, D_MODEL), lambda j, te, nu: (te[j], 0, 0))],
        out_specs=pl.BlockSpec((TM_E, D_MODEL), lambda j, te, nu: (j, 0)),
    )
    return pl.pallas_call(
        _moe_body,
        out_shape=jax.ShapeDtypeStruct((n_tiles * TM_E, D_MODEL), _F32),
        grid_spec=grid_spec,
        compiler_params=_params("arbitrary"),
        name="moe",
    )(tile_expert, n_used, xs, cw, w_gate, w_up, w_down)


def _ple_body(h1_ref, ya_ref, yb_ref, p_ref, g_ref, wpg_ref, bpg_ref, wple_ref, fg_ref, o_ref):
    h2 = h1_ref[...] + ya_ref[...] + yb_ref[...]
    hn = _rms(h2, g_ref[...]).astype(_BF16)
    gate = jax.nn.sigmoid(_dot(hn, wpg_ref[...]) + bpg_ref[...])
    pe = _dot(p_ref[...].astype(_BF16), wple_ref[...])
    o_ref[...] = _rms(h2 + gate * pe, fg_ref[...])


def _ple(h1, ya, yb, p, g, wpg, bpg, wple, fg, tm, row0):
    n = p.shape[0]
    off = row0 // tm
    src = pl.BlockSpec((tm, D_MODEL), lambda i: (i + off, 0))
    return pl.pallas_call(
        _ple_body,
        out_shape=jax.ShapeDtypeStruct((n, D_MODEL), _F32),
        grid=(n // tm,),
        in_specs=[src, src, src, pl.BlockSpec((tm, D_PLE), lambda i: (i, 0)), _const_spec((1, D_MODEL)),
                  _const_spec((D_MODEL, D_MODEL)), _const_spec((1, D_MODEL)), _const_spec((D_PLE, D_MODEL)),
                  _const_spec((1, D_MODEL))],
        out_specs=pl.BlockSpec((tm, D_MODEL), lambda i: (i, 0)),
        compiler_params=_params("arbitrary"),
        name="ple",
    )(h1, ya, yb, p, g, wpg, bpg, wple, fg)


def _route(logits):
    n = logits.shape[0]
    g_logits = logits[:, :N_GROUPS]
    e_logits = logits[:, N_GROUPS:N_GROUPS + N_EXPERTS].reshape(n, N_GROUPS, EPG)
    g_idx = jnp.argmax(g_logits, axis=-1)
    g_w = jnp.take_along_axis(jax.nn.softmax(g_logits, axis=-1), g_idx[:, None], axis=-1)
    e_in = jnp.take_along_axis(e_logits, g_idx[:, None, None], axis=1)[:, 0]
    top_v, top_i = lax.top_k(e_in, 2)
    weights = g_w * jax.nn.softmax(top_v, axis=-1)
    expert = (g_idx[:, None] * EPG + top_i).astype(jnp.int32)
    return expert, weights


def _schedule(expert, weights, n_tiles):
    keys = expert.reshape(-1)
    n_pairs = keys.shape[0]
    order = jnp.argsort(keys, stable=True).astype(jnp.int32)
    counts = jnp.sum((keys[:, None] == jnp.arange(N_EXPERTS, dtype=jnp.int32)[None, :]).astype(jnp.int32), axis=0)
    tiles_per = (counts + TM_E - 1) // TM_E
    tiles_end = jnp.cumsum(tiles_per)
    start_pad = (tiles_end - tiles_per) * TM_E
    start_raw = jnp.cumsum(counts) - counts
    n_used = tiles_end[-1]
    tile_ids = jnp.arange(n_tiles, dtype=jnp.int32)
    te = jnp.minimum(jnp.searchsorted(tiles_end, tile_ids, side="right"), N_EXPERTS - 1).astype(jnp.int32)
    te = jnp.where(tile_ids < n_used, te, te[jnp.maximum(n_used - 1, 0)])
    slot = jnp.arange(n_tiles * TM_E, dtype=jnp.int32)
    e_slot = te[slot // TM_E]
    rank = slot - start_pad[e_slot]
    valid = (slot // TM_E < n_used) & (rank < counts[e_slot])
    pair = order[jnp.clip(start_raw[e_slot] + rank, 0, n_pairs - 1)]
    src_tok = jnp.where(valid, pair // 2, 0).astype(jnp.int32)
    cw = jnp.where(valid, weights.reshape(-1)[pair], 0.0).astype(_F32)
    sorted_keys = keys[order]
    dest_sorted = start_pad[sorted_keys] + jnp.arange(n_pairs, dtype=jnp.int32) - start_raw[sorted_keys]
    dest = jnp.zeros((n_pairs,), jnp.int32).at[order].set(dest_sorted.astype(jnp.int32), unique_indices=True)
    return te, n_used.astype(jnp.int32).reshape(1), src_tok, cw[:, None], dest.reshape(-1, 2)


def _pad_to(idx, mult):
    pad = (-idx.shape[0]) % mult
    return jnp.concatenate([idx, jnp.zeros((pad,), idx.dtype)]) if pad else idx


def kernel(x_prompt, x_sample, state_conf_conv, state_lru_conv, state_lru_h, p_prompt, p_sample, norm1_g, w_in, b_in, conf_dw_w, conf_dw_b, conf_ln_g, conf_ln_b, lru_conv_w, lru_conv_b, lru_wa, lru_ba, lru_wx, lru_bx, lru_lambda, w_out, norm2_g, w_grp, b_grp, w_rt, b_rt, w_gate, w_up, w_down, ple_norm_g, w_ple, w_pg, b_pg, final_g):
    batch, seq, _ = x_prompt.shape
    nb = x_sample.shape[0]
    n_p = batch * seq
    n_all = n_p + nb
    layer = 0

    row = lambda v: v.reshape(1, -1)
    w_in_b = w_in[layer].astype(_BF16)
    wc_b = w_out[layer, :D_CONF].astype(_BF16)
    wr_b = w_out[layer, D_CONF:].astype(_BF16)
    w_router = jnp.concatenate(
        [w_grp[layer], w_rt[layer], jnp.zeros((D_MODEL, LANES - N_GROUPS - N_EXPERTS), _F32)], axis=1).astype(_BF16)
    b_router = jnp.concatenate([b_grp[layer], b_rt[layer], jnp.zeros((LANES - N_GROUPS - N_EXPERTS,), _F32)])
    w_gates = jnp.concatenate([lru_wa[layer], lru_wx[layer]], axis=-1).astype(_BF16)
    b_gates = jnp.concatenate([lru_ba[layer], lru_bx[layer]], axis=-1)
    w_pg_b = w_pg[layer].astype(_BF16)
    w_ple_b = w_ple[layer].astype(_BF16)
    conf_args = (conf_dw_w[layer], row(conf_dw_b[layer]), row(conf_ln_g[layer]), row(conf_ln_b[layer]))
    lru_args = (lru_conv_w[layer], row(lru_conv_b[layer]), w_gates, b_gates, row(lru_lambda[layer]))
    out_args = (wc_b, wr_b, row(norm2_g[layer]), w_router, row(b_router))
    ple_args = (row(ple_norm_g[layer]), w_pg_b, row(b_pg[layer]), w_ple_b, row(final_g))

    xp = x_prompt.reshape(n_p, D_MODEL)
    u_p, lx_p, lg_p = _inproj(xp, row(norm1_g[layer]), w_in_b, row(b_in[layer]), TM)
    c_p = _conf_prompt(u_p, *conf_args, batch, seq)
    r_p, hl_p = _lru_prompt(lx_p, lg_p, *lru_args, batch, seq)

    xs = x_sample.reshape(nb, D_MODEL)
    conf_buf = state_conf_conv[layer]
    lru_buf = state_lru_conv[layer]
    u_s, lx_s, lg_s = _inproj(xs, row(norm1_g[layer]), w_in_b, row(b_in[layer]), nb)
    c_s = _conf_step(conf_buf, u_s, *conf_args)
    r_s, hn_s = _lru_step(lru_buf[:, 0], lru_buf[:, 1], lru_buf[:, 2], lx_s, state_lru_h[layer], lg_s, *lru_args)
    h1, xn, logits = _outproj(c_p, r_p, xp, c_s, r_s, xs, *out_args, TM)

    n_tiles = (2 * n_all + TM_E - 1) // TM_E + N_EXPERTS
    expert, weights = _route(logits[:n_all])
    tile_expert, n_used, src_tok, cw, dest = _schedule(expert, weights, n_tiles)
    x_sorted = _gather_rows(xn, _pad_to(src_tok, GATHER_CH))
    y_sorted = _moe(tile_expert, n_used, x_sorted, cw, w_gate[layer], w_up[layer], w_down[layer])
    ya = _gather_rows(y_sorted, _pad_to(dest[:, 0], GATHER_CH))
    yb = _gather_rows(y_sorted, _pad_to(dest[:, 1], GATHER_CH))

    y_p = _ple(h1, ya, yb, p_prompt[layer].reshape(n_p, D_PLE), *ple_args, TM, 0)
    y_s = _ple(h1, ya, yb, p_sample[layer].reshape(nb, D_PLE), *ple_args, nb, n_p)

    u_p3 = u_p.reshape(batch, seq, D_CONF)
    lx_p3 = lx_p.reshape(batch, seq, D_LRU)
    return (
        y_p.reshape(batch, seq, D_MODEL),
        y_s.reshape(nb, 1, D_MODEL),
        u_p3[None, :, seq - (CONF_K - 1):],
        jnp.concatenate([conf_buf[:, 1:], u_s[:, None]], axis=1)[None],
        lx_p3[None, :, seq - (LRU_K - 1):],
        jnp.concatenate([lru_buf[:, 1:], lx_s[:, None]], axis=1)[None],
        hl_p.reshape(1, batch, D_LRU),
        hn_s[None],
    )
```

```python
import functools

import jax
import jax.numpy as jnp
from jax import lax
from jax.experimental import pallas as pl
from jax.experimental.pallas import tpu as pltpu

D_MODEL = 2048
D_CONF = 1024
D_LRU = 1024
N_HEADS = 8
HEAD = 128
CONF_K = 31
LRU_K = 4
LRU_C = 8.0
N_GROUPS = 4
EPG = 8
N_EXPERTS = 32
D_EXPERT = 512
D_PLE = 256
EPS = 1e-6

LANES = 128
SUBLANES = 8
VMEM_LIMIT = 56 * 1024 * 1024

TM = 512
TM_E = 256
CONF_TC = 64
LRU_TC = 256
LRU_PITCH = LRU_TC + SUBLANES

_BF16 = jnp.bfloat16
_F32 = jnp.float32


def _params(*sem):
    return pltpu.CompilerParams(dimension_semantics=sem, vmem_limit_bytes=VMEM_LIMIT)


def _const_spec(shape):
    nd = len(shape)
    return pl.BlockSpec(shape, lambda *_: (0,) * nd, pipeline_mode=pl.Buffered(1))


def _rms(x, g):
    return x * lax.rsqrt(jnp.mean(x * x, axis=-1, keepdims=True) + EPS) * g


def _dot(a, b):
    return jnp.dot(a, b, preferred_element_type=_F32)


def _inproj_body(x_ref, g_ref, w_ref, b_ref, u_ref, lx_ref, lg_ref):
    xn = _rms(x_ref[...], g_ref[...]).astype(_BF16)

    def proj(k):
        sl = slice(k * D_CONF, (k + 1) * D_CONF)
        return _dot(xn, w_ref[:, sl]) + b_ref[:, sl]

    u_ref[...] = proj(0) * jax.nn.sigmoid(proj(1))
    lx_ref[...] = proj(2)
    lg_ref[...] = jax.nn.gelu(proj(3))


def _inproj(x, g, w, b, tm):
    n = x.shape[0]
    out = jax.ShapeDtypeStruct((n, D_CONF), _F32)
    tok = lambda width: pl.BlockSpec((tm, width), lambda i: (i, 0))
    return pl.pallas_call(
        _inproj_body,
        out_shape=(out, out, out),
        grid=(n // tm,),
        in_specs=[tok(D_MODEL), _const_spec((1, D_MODEL)), _const_spec((D_MODEL, 4 * D_CONF)),
                  _const_spec((1, 4 * D_CONF))],
        out_specs=(tok(D_CONF), tok(D_CONF), tok(D_CONF)),
        compiler_params=_params("arbitrary"),
        name="inproj",
    )(x, g, w, b)


def _layernorm_silu(conv, g, b):
    mu = jnp.mean(conv, axis=-1, keepdims=True)
    cen = conv - mu
    var = jnp.mean(cen * cen, axis=-1, keepdims=True)
    y = cen * lax.rsqrt(var + EPS) * g + b
    return y * jax.nn.sigmoid(y)


def _conf_body(u_ref, w_ref, b_ref, g_ref, lb_ref, c_ref, s_ref, *, seq):
    head = 4 * SUBLANES
    s_ref[0:head, :] = jnp.zeros((head, D_CONF), _F32)
    s_ref[head:head + seq, :] = u_ref[...]
    s_ref[head + seq:, :] = jnp.zeros((SUBLANES, D_CONF), _F32)
    tc = CONF_TC

    def chunk(ci, carry):
        t0 = pl.multiple_of(ci * tc, tc)
        win = s_ref[pl.ds(t0, tc + 5 * SUBLANES), :]
        out = None
        for r in range(SUBLANES):
            part = None
            for q in range(5):
                m = SUBLANES * q + r
                if 2 <= m <= CONF_K + 1:
                    term = win[SUBLANES * q:SUBLANES * q + tc + SUBLANES, :] * w_ref[m - 2:m - 1, :]
                    part = term if part is None else part + term
            shifted = part[r:r + tc, :]
            out = shifted if out is None else out + shifted
        c = _layernorm_silu(out + b_ref[...], g_ref[...], lb_ref[...])
        c_ref[pl.ds(t0, tc), :] = c.astype(_BF16)
        return carry

    lax.fori_loop(0, seq // tc, chunk, 0)


def _conf_prompt(u, w, b, g, lb, batch, seq):
    return pl.pallas_call(
        functools.partial(_conf_body, seq=seq),
        out_shape=jax.ShapeDtypeStruct((batch * seq, D_CONF), _BF16),
        grid=(batch,),
        in_specs=[pl.BlockSpec((seq, D_CONF), lambda i: (i, 0)), _const_spec((CONF_K, D_CONF)),
                  _const_spec((1, D_CONF)), _const_spec((1, D_CONF)), _const_spec((1, D_CONF))],
        out_specs=pl.BlockSpec((seq, D_CONF), lambda i: (i, 0)),
        scratch_shapes=[pltpu.VMEM((seq + 5 * SUBLANES, D_CONF), _F32)],
        compiler_params=_params("arbitrary"),
        name="conf_prompt",
    )(u, w, b, g, lb)


def _conf_step_body(buf_ref, u_ref, w_ref, b_ref, g_ref, lb_ref, c_ref):
    w = w_ref[...]
    conv = jnp.sum(buf_ref[...] * w[None, :CONF_K - 1, :], axis=1) + u_ref[...] * w[CONF_K - 1:CONF_K, :]
    c_ref[...] = _layernorm_silu(conv + b_ref[...], g_ref[...], lb_ref[...]).astype(_BF16)


def _conf_step(buf, u, w, b, g, lb):
    nb = buf.shape[0]
    bb = 16
    return pl.pallas_call(
        _conf_step_body,
        out_shape=jax.ShapeDtypeStruct((nb, D_CONF), _BF16),
        grid=(nb // bb,),
        in_specs=[pl.BlockSpec((bb, CONF_K - 1, D_CONF), lambda i: (i, 0, 0)),
                  pl.BlockSpec((bb, D_CONF), lambda i: (i, 0)), _const_spec((CONF_K, D_CONF)),
                  _const_spec((1, D_CONF)), _const_spec((1, D_CONF)), _const_spec((1, D_CONF))],
        out_specs=pl.BlockSpec((bb, D_CONF), lambda i: (i, 0)),
        compiler_params=_params("arbitrary"),
        name="conf_step",
    )(buf, u, w, b, g, lb)


def _lru_gates(lx, wg_ref, bg_ref, lam_ref, head):
    sl = slice(head * HEAD, (head + 1) * HEAD)
    lxh = lx[:, sl]
    z = _dot(lxh.astype(_BF16), wg_ref[head]) + bg_ref[head:head + 1, :]
    r = jax.nn.sigmoid(z[:, :HEAD])
    i = jax.nn.sigmoid(z[:, HEAD:])
    log_a = -LRU_C * r * jax.nn.softplus(-lam_ref[:, sl])
    a = jnp.exp(log_a)
    return a, jnp.sqrt(-jnp.tanh(log_a) * (a * a + 1.0)) * i * lxh


def _lru_body(lx_ref, lg_ref, cw_ref, cb_ref, wg_ref, bg_ref, lam_ref, r_ref, hl_ref,
              halo_ref, h_ref, a_ref, u_ref, hs_ref):
    j = pl.program_id(1)
    tc = LRU_TC

    @pl.when(j == 0)
    def _():
        halo_ref[...] = jnp.zeros_like(halo_ref)
        h_ref[...] = jnp.zeros_like(h_ref)

    cur = lx_ref[...]
    ext = jnp.concatenate([halo_ref[...], cur], axis=0)
    lx = cb_ref[...] + cw_ref[LRU_K - 1:LRU_K, :] * cur
    for back in range(1, LRU_K):
        lx = lx + cw_ref[LRU_K - 1 - back:LRU_K - back, :] * ext[SUBLANES - back:SUBLANES - back + tc, :]
    halo_ref[...] = cur[tc - SUBLANES:, :]

    for head in range(N_HEADS):
        a, u = _lru_gates(lx, wg_ref, bg_ref, lam_ref, head)
        a_ref[head * LRU_PITCH:head * LRU_PITCH + tc, :] = a
        u_ref[head * LRU_PITCH:head * LRU_PITCH + tc, :] = u

    def step(t, h):
        rows = pl.ds(t, N_HEADS, stride=LRU_PITCH)
        h = a_ref[rows, :] * h + u_ref[rows, :]
        hs_ref[rows, :] = h
        return h

    h_ref[...] = lax.fori_loop(0, tc, step, h_ref[...], unroll=8)

    for head in range(N_HEADS):
        sl = slice(head * HEAD, (head + 1) * HEAD)
        hs = hs_ref[head * LRU_PITCH:head * LRU_PITCH + tc, :]
        r_ref[:, sl] = (hs * lg_ref[:, sl]).astype(_BF16)

    @pl.when(j == pl.num_programs(1) - 1)
    def _():
        hl_ref[0] = h_ref[...]


def _lru_prompt(lx, lg, cw, cb, wg, bg, lam, batch, seq):
    nchunk = seq // LRU_TC
    tok = pl.BlockSpec((LRU_TC, D_LRU), lambda b, j: (b * nchunk + j, 0))
    pitch_rows = N_HEADS * LRU_PITCH
    return pl.pallas_call(
        _lru_body,
        out_shape=(jax.ShapeDtypeStruct((batch * seq, D_LRU), _BF16),
                   jax.ShapeDtypeStruct((batch, N_HEADS, HEAD), _F32)),
        grid=(batch, nchunk),
        in_specs=[tok, tok, _const_spec((LRU_K, D_LRU)), _const_spec((1, D_LRU)),
                  _const_spec((N_HEADS, HEAD, 2 * HEAD)), _const_spec((N_HEADS, 2 * HEAD)),
                  _const_spec((1, D_LRU))],
        out_specs=(tok, pl.BlockSpec((1, N_HEADS, HEAD), lambda b, j: (b, 0, 0))),
        scratch_shapes=[pltpu.VMEM((SUBLANES, D_LRU), _F32), pltpu.VMEM((N_HEADS, HEAD), _F32),
                        pltpu.VMEM((pitch_rows, HEAD), _F32), pltpu.VMEM((pitch_rows, HEAD), _F32),
                        pltpu.VMEM((pitch_rows, HEAD), _F32)],
        compiler_params=_params("arbitrary", "arbitrary"),
        name="lru_prompt",
    )(lx, lg, cw, cb, wg, bg, lam)


def _lru_step_body(b0_ref, b1_ref, b2_ref, lx_ref, h0_ref, lg_ref, cw_ref, cb_ref, wg_ref, bg_ref, lam_ref,
                   r_ref, hn_ref):
    lx = (cb_ref[...] + cw_ref[0:1, :] * b0_ref[...] + cw_ref[1:2, :] * b1_ref[...]
          + cw_ref[2:3, :] * b2_ref[...] + cw_ref[3:4, :] * lx_ref[...])
    for head in range(N_HEADS):
        sl = slice(head * HEAD, (head + 1) * HEAD)
        a, u = _lru_gates(lx, wg_ref, bg_ref, lam_ref, head)
        h = a * h0_ref[:, sl] + u
        hn_ref[:, sl] = h
        r_ref[:, sl] = (h * lg_ref[:, sl]).astype(_BF16)


def _lru_step(b0, b1, b2, lx, h0, lg, cw, cb, wg, bg, lam):
    nb = lx.shape[0]
    full = _const_spec((nb, D_LRU))
    return pl.pallas_call(
        _lru_step_body,
        out_shape=(jax.ShapeDtypeStruct((nb, D_LRU), _BF16), jax.ShapeDtypeStruct((nb, D_LRU), _F32)),
        grid=(1,),
        in_specs=[full] * 6 + [_const_spec((LRU_K, D_LRU)), _const_spec((1, D_LRU)),
                               _const_spec((N_HEADS, HEAD, 2 * HEAD)), _const_spec((N_HEADS, 2 * HEAD)),
                               _const_spec((1, D_LRU))],
        out_specs=(pl.BlockSpec((nb, D_LRU), lambda i: (0, 0)), pl.BlockSpec((nb, D_LRU), lambda i: (0, 0))),
        compiler_params=_params("arbitrary"),
        name="lru_step",
    )(b0, b1, b2, lx, h0, lg, cw, cb, wg, bg, lam)


R_E1, R_E2, R_W1, R_W2, R_RANK1, R_RANK2 = range(6)


def _route_tile(logits, seen):
    m = logits.shape[0]
    lane = lax.broadcasted_iota(jnp.int32, (m, LANES), 1).astype(_F32)
    neg = jnp.float32(-jnp.inf)

    def first_max(vals):
        vmax = jnp.max(vals, axis=1, keepdims=True)
        return vmax, jnp.min(jnp.where(vals == vmax, lane, float(LANES)), axis=1, keepdims=True)

    in_groups = lane < N_GROUPS
    g_max, g_idx = first_max(jnp.where(in_groups, logits, neg))
    g_den = jnp.sum(jnp.where(in_groups, jnp.exp(logits - g_max), 0.0), axis=1, keepdims=True)
    lo = N_GROUPS + EPG * g_idx
    e_logits = jnp.where((lane >= lo) & (lane < lo + EPG), logits, neg)
    v1, i1 = first_max(e_logits)
    v2, i2 = first_max(jnp.where(lane == i1, neg, e_logits))
    t = jnp.exp(v2 - v1)
    w1 = 1.0 / (g_den * (1.0 + t))
    w2 = w1 * t
    e1 = i1 - N_GROUPS
    e2 = i2 - N_GROUPS

    is1 = lane == e1
    is2 = lane == e2
    onehot = jnp.where(is1 | is2, 1.0, 0.0)
    earlier = (lax.broadcasted_iota(jnp.int32, (m, m), 1) < lax.broadcasted_iota(jnp.int32, (m, m), 0))
    before = seen + _dot(jnp.where(earlier, 1.0, 0.0).astype(_BF16), onehot.astype(_BF16))
    rank1 = jnp.sum(jnp.where(is1, before, 0.0), axis=1, keepdims=True)
    rank2 = jnp.sum(jnp.where(is2, before, 0.0), axis=1, keepdims=True)

    rec = jnp.zeros((m, LANES), _F32)
    for pos, val in ((R_E1, e1), (R_E2, e2), (R_W1, w1), (R_W2, w2), (R_RANK1, rank1), (R_RANK2, rank2)):
        rec = jnp.where(lane == pos, val, rec)
    return rec, seen + jnp.sum(onehot, axis=0, keepdims=True)


def _outproj_body(cp_ref, rp_ref, xp_ref, cs_ref, rs_ref, xs_ref, wc_ref, wr_ref, g_ref, wrt_ref, brt_ref,
                  h1_ref, xn_ref, rec_ref, cnt_ref, seen_ref, *, n_ptiles, nb):
    i = pl.program_id(0)

    @pl.when(i == 0)
    def _():
        seen_ref[...] = jnp.zeros_like(seen_ref)

    def mix(c, r, x):
        h1 = x + _dot(c, wc_ref[...]) + _dot(r, wr_ref[...])
        xn = _rms(h1, g_ref[...])
        logits = _dot(xn.astype(_BF16), wrt_ref[...]) + brt_ref[...]
        rec, seen = _route_tile(logits, seen_ref[...])
        seen_ref[...] = seen
        return h1, xn, rec

    @pl.when(i < n_ptiles)
    def _():
        h1_ref[...], xn_ref[...], rec_ref[...] = mix(cp_ref[...], rp_ref[...], xp_ref[...])

    @pl.when(i == n_ptiles)
    def _():
        for ref, val in zip((h1_ref, xn_ref, rec_ref), mix(cs_ref[...], rs_ref[...], xs_ref[...])):
            ref[:nb, :] = val
            ref[nb:, :] = jnp.zeros((ref.shape[0] - nb, ref.shape[1]), _F32)

    cnt_ref[...] = jnp.broadcast_to(seen_ref[...], cnt_ref.shape)


def _outproj(cp, rp, xp, cs, rs, xs, wc, wr, g, wrt, brt, tm):
    n_ptiles = xp.shape[0] // tm
    nb = xs.shape[0]
    rows = (n_ptiles + 1) * tm
    last = n_ptiles - 1
    tok = lambda width: pl.BlockSpec((tm, width), lambda i: (jnp.minimum(i, last), 0))
    out_tok = lambda width: pl.BlockSpec((tm, width), lambda i: (i, 0))
    return pl.pallas_call(
        functools.partial(_outproj_body, n_ptiles=n_ptiles, nb=nb),
        out_shape=(jax.ShapeDtypeStruct((rows, D_MODEL), _F32), jax.ShapeDtypeStruct((rows, D_MODEL), _F32),
                   jax.ShapeDtypeStruct((rows, LANES), _F32), jax.ShapeDtypeStruct((SUBLANES, LANES), _F32)),
        grid=(n_ptiles + 1,),
        in_specs=[tok(D_CONF), tok(D_LRU), tok(D_MODEL), _const_spec((nb, D_CONF)), _const_spec((nb, D_LRU)),
                  _const_spec((nb, D_MODEL)), _const_spec((D_CONF, D_MODEL)), _const_spec((D_LRU, D_MODEL)),
                  _const_spec((1, D_MODEL)), _const_spec((D_MODEL, LANES)), _const_spec((1, LANES))],
        out_specs=(out_tok(D_MODEL), out_tok(D_MODEL), out_tok(LANES),
                   pl.BlockSpec((SUBLANES, LANES), lambda i: (0, 0))),
        scratch_shapes=[pltpu.VMEM((1, LANES), _F32)],
        compiler_params=_params("arbitrary"),
        name="outproj",
    )(cp, rp, xp, cs, rs, xs, wc, wr, g, wrt, brt)


def _moe_body(te_ref, nv_ref, nu_ref, tbl_ref, xn_hbm, wg_ref, wu_ref, wd_ref, y_hbm,
              xbuf, ybuf, wg_b, wu_b, wd_b, gsem, ssem, *, n_all):
    j = pl.program_id(0)
    n_used = nu_ref[0]
    cur = lax.rem(j, 2)

    def token_of(tile, r):
        v = tbl_ref[tile * TM_E + r]
        k = (v >= n_all).astype(jnp.int32)
        return k, v - k * n_all

    def start_gather(tile, buf):
        def body(r, carry):
            _, tok = token_of(tile, r)
            pltpu.make_async_copy(xn_hbm.at[pl.ds(tok, 1), :], xbuf.at[buf, pl.ds(r, 1), :], gsem.at[buf]).start()
            return carry
        lax.fori_loop(0, nv_ref[tile], body, 0)

    def wait_rows(tile, bulk_copy, row_copy):
        n = nv_ref[tile]
        n8 = pl.multiple_of((n // SUBLANES) * SUBLANES, SUBLANES)

        @pl.when(n8 > 0)
        def _():
            bulk_copy(pl.ds(0, n8)).wait()

        def body(r, carry):
            row_copy(pl.ds(r, 1)).wait()
            return carry
        lax.fori_loop(n8, n, body, 0)

    def wait_gather(tile, buf):
        copy = lambda rows: pltpu.make_async_copy(xn_hbm.at[rows, :], xbuf.at[buf, rows, :], gsem.at[buf])
        wait_rows(tile, copy, copy)

    def start_scatter(tile, buf):
        def body(r, carry):
            k, tok = token_of(tile, r)
            pltpu.make_async_copy(ybuf.at[buf, pl.ds(r, 1), :], y_hbm.at[k, pl.ds(tok, 1), :], ssem.at[buf]).start()
            return carry
        lax.fori_loop(0, nv_ref[tile], body, 0)

    def wait_scatter(tile, buf):
        copy = lambda rows: pltpu.make_async_copy(ybuf.at[buf, rows, :], y_hbm.at[0, rows, :], ssem.at[buf])
        wait_rows(tile, copy, copy)

    @pl.when(j == 0)
    def _():
        xbuf[...] = jnp.zeros_like(xbuf)
        start_gather(0, 0)

    @pl.when(j + 1 < n_used)
    def _():
        start_gather(j + 1, 1 - cur)

    @pl.when((j >= 2) & (j - 2 < n_used))
    def _():
        wait_scatter(j - 2, cur)

    @pl.when(j < n_used)
    def _():
        wait_gather(j, cur)

        @pl.when((j == 0) | (te_ref[j] != te_ref[jnp.maximum(j - 1, 0)]))
        def _():
            wg_b[...] = wg_ref[0].astype(_BF16)
            wu_b[...] = wu_ref[0].astype(_BF16)
            wd_b[...] = wd_ref[0].astype(_BF16)

        x = xbuf[cur].astype(_BF16)
        g = _dot(x, wg_b[...])
        u = _dot(x, wu_b[...])
        hid = g * jax.nn.sigmoid(g) * u
        ybuf[cur] = _dot(hid.astype(_BF16), wd_b[...])
        start_scatter(j, cur)


def _moe(tile_expert, tile_rows, n_used, table, xn, w_gate, w_up, w_down, n_all):
    n_tiles = tile_expert.shape[0]
    by_expert = lambda shape: pl.BlockSpec(shape, lambda j, te, nv, nu, tbl: (te[j], 0, 0))
    grid_spec = pltpu.PrefetchScalarGridSpec(
        num_scalar_prefetch=4,
        grid=(n_tiles,),
        in_specs=[pl.BlockSpec(memory_space=pl.ANY), by_expert((1, D_MODEL, D_EXPERT)),
                  by_expert((1, D_MODEL, D_EXPERT)), by_expert((1, D_EXPERT, D_MODEL))],
        out_specs=pl.BlockSpec(memory_space=pl.ANY),
        scratch_shapes=[pltpu.VMEM((2, TM_E, D_MODEL), _F32), pltpu.VMEM((2, TM_E, D_MODEL), _F32),
                        pltpu.VMEM((D_MODEL, D_EXPERT), _BF16), pltpu.VMEM((D_MODEL, D_EXPERT), _BF16),
                        pltpu.VMEM((D_EXPERT, D_MODEL), _BF16),
                        pltpu.SemaphoreType.DMA((2,)), pltpu.SemaphoreType.DMA((2,))],
    )
    return pl.pallas_call(
        functools.partial(_moe_body, n_all=n_all),
        out_shape=jax.ShapeDtypeStruct((2, n_all, D_MODEL), _F32),
        grid_spec=grid_spec,
        compiler_params=_params("arbitrary"),
        name="moe",
    )(tile_expert, tile_rows, n_used, table, xn, w_gate, w_up, w_down)


def _ple_body(h1_ref, y1_ref, y2_ref, rec_ref, p_ref, g_ref, wpg_ref, bpg_ref, wple_ref, fg_ref, o_ref):
    rec = rec_ref[...]
    h2 = h1_ref[...] + rec[:, R_W1:R_W1 + 1] * y1_ref[...] + rec[:, R_W2:R_W2 + 1] * y2_ref[...]
    hn = _rms(h2, g_ref[...]).astype(_BF16)
    gate = jax.nn.sigmoid(_dot(hn, wpg_ref[...]) + bpg_ref[...])
    pe = _dot(p_ref[...].astype(_BF16), wple_ref[...])
    o_ref[...] = _rms(h2 + gate * pe, fg_ref[...])


def _ple(h1, y, rec, p, g, wpg, bpg, wple, fg, tm, row0):
    n = p.shape[0]
    off = row0 // tm
    tok = lambda width: pl.BlockSpec((tm, width), lambda i: (i + off, 0))
    expert_out = lambda k: pl.BlockSpec((None, tm, D_MODEL), lambda i: (k, i + off, 0))
    return pl.pallas_call(
        _ple_body,
        out_shape=jax.ShapeDtypeStruct((n, D_MODEL), _F32),
        grid=(n // tm,),
        in_specs=[tok(D_MODEL), expert_out(0), expert_out(1), tok(LANES), pl.BlockSpec((tm, D_PLE), lambda i: (i, 0)),
                  _const_spec((1, D_MODEL)), _const_spec((D_MODEL, D_MODEL)), _const_spec((1, D_MODEL)),
                  _const_spec((D_PLE, D_MODEL)), _const_spec((1, D_MODEL))],
        out_specs=pl.BlockSpec((tm, D_MODEL), lambda i: (i, 0)),
        compiler_params=_params("arbitrary"),
        name="ple",
    )(h1, y, y, rec, p, g, wpg, bpg, wple, fg)


def _schedule(rec, counts_row, n_all, n_tiles):
    ids = jnp.arange(N_EXPERTS, dtype=jnp.int32)
    expert = rec[:n_all, R_E1:R_E2 + 1].astype(jnp.int32)
    rank = rec[:n_all, R_RANK1:R_RANK2 + 1].astype(jnp.int32)
    counts = counts_row[0, :N_EXPERTS].astype(jnp.int32)
    tiles_per = (counts + TM_E - 1) // TM_E
    tiles_end = jnp.cumsum(tiles_per)
    tile_start = tiles_end - tiles_per
    n_used = tiles_end[-1]
    slot = jnp.sum(jnp.where(expert[..., None] == ids, tile_start * TM_E, 0), axis=-1) + rank
    pair_id = jnp.arange(n_all, dtype=jnp.int32)[:, None] + n_all * jnp.arange(2, dtype=jnp.int32)[None, :]
    table = jnp.zeros((n_tiles * TM_E,), jnp.int32).at[slot.reshape(-1)].set(pair_id.reshape(-1), unique_indices=True)
    tile_ids = jnp.arange(n_tiles, dtype=jnp.int32)
    te = jnp.minimum(jnp.sum((tiles_end[None, :] <= tile_ids[:, None]).astype(jnp.int32), axis=1), N_EXPERTS - 1)
    mine = te[:, None] == ids[None, :]
    left = jnp.sum(jnp.where(mine, counts, 0), axis=1) - (tile_ids - jnp.sum(jnp.where(mine, tile_start, 0), axis=1)) * TM_E
    rows = jnp.where(tile_ids < n_used, jnp.clip(left, 0, TM_E), 0).astype(jnp.int32)
    last_expert = jnp.sum(jnp.where(tile_ids == n_used - 1, te, 0))
    te = jnp.where(tile_ids < n_used, te, last_expert).astype(jnp.int32)
    return te, rows, n_used.astype(jnp.int32).reshape(1), table


def kernel(x_prompt, x_sample, state_conf_conv, state_lru_conv, state_lru_h, p_prompt, p_sample, norm1_g, w_in, b_in, conf_dw_w, conf_dw_b, conf_ln_g, conf_ln_b, lru_conv_w, lru_conv_b, lru_wa, lru_ba, lru_wx, lru_bx, lru_lambda, w_out, norm2_g, w_grp, b_grp, w_rt, b_rt, w_gate, w_up, w_down, ple_norm_g, w_ple, w_pg, b_pg, final_g):
    batch, seq, _ = x_prompt.shape
    nb = x_sample.shape[0]
    n_p = batch * seq
    n_all = n_p + nb
    layer = 0

    row = lambda v: v.reshape(1, -1)
    w_in_b = w_in[layer].astype(_BF16)
    wc_b = w_out[layer, :D_CONF].astype(_BF16)
    wr_b = w_out[layer, D_CONF:].astype(_BF16)
    w_router = jnp.concatenate(
        [w_grp[layer], w_rt[layer], jnp.zeros((D_MODEL, LANES - N_GROUPS - N_EXPERTS), _F32)], axis=1).astype(_BF16)
    b_router = jnp.concatenate([b_grp[layer], b_rt[layer], jnp.zeros((LANES - N_GROUPS - N_EXPERTS,), _F32)])
    w_gates = jnp.concatenate([lru_wa[layer], lru_wx[layer]], axis=-1).astype(_BF16)
    b_gates = jnp.concatenate([lru_ba[layer], lru_bx[layer]], axis=-1)
    w_pg_b = w_pg[layer].astype(_BF16)
    w_ple_b = w_ple[layer].astype(_BF16)
    conf_args = (conf_dw_w[layer], row(conf_dw_b[layer]), row(conf_ln_g[layer]), row(conf_ln_b[layer]))
    lru_args = (lru_conv_w[layer], row(lru_conv_b[layer]), w_gates, b_gates, row(lru_lambda[layer]))
    out_args = (wc_b, wr_b, row(norm2_g[layer]), w_router, row(b_router))
    ple_args = (row(ple_norm_g[layer]), w_pg_b, row(b_pg[layer]), w_ple_b, row(final_g))

    xp = x_prompt.reshape(n_p, D_MODEL)
    u_p, lx_p, lg_p = _inproj(xp, row(norm1_g[layer]), w_in_b, row(b_in[layer]), TM)
    c_p = _conf_prompt(u_p, *conf_args, batch, seq)
    r_p, hl_p = _lru_prompt(lx_p, lg_p, *lru_args, batch, seq)

    xs = x_sample.reshape(nb, D_MODEL)
    conf_buf = state_conf_conv[layer]
    lru_buf = state_lru_conv[layer]
    u_s, lx_s, lg_s = _inproj(xs, row(norm1_g[layer]), w_in_b, row(b_in[layer]), nb)
    c_s = _conf_step(conf_buf, u_s, *conf_args)
    r_s, hn_s = _lru_step(lru_buf[:, 0], lru_buf[:, 1], lru_buf[:, 2], lx_s, state_lru_h[layer], lg_s, *lru_args)
    h1, xn, rec, counts = _outproj(c_p, r_p, xp, c_s, r_s, xs, *out_args, TM)

    n_tiles = (2 * n_all + TM_E - 1) // TM_E + N_EXPERTS + 2
    tile_expert, tile_rows, n_used, table = _schedule(rec, counts, n_all, n_tiles)
    y = _moe(tile_expert, tile_rows, n_used, table, xn, w_gate[layer], w_up[layer], w_down[layer], n_all)

    y_p = _ple(h1, y, rec, p_prompt[layer].reshape(n_p, D_PLE), *ple_args, TM, 0)
    y_s = _ple(h1, y, rec, p_sample[layer].reshape(nb, D_PLE), *ple_args, nb, n_p)

    u_p3 = u_p.reshape(batch, seq, D_CONF)
    lx_p3 = lx_p.reshape(batch, seq, D_LRU)
    return (
        y_p.reshape(batch, seq, D_MODEL),
        y_s.reshape(nb, 1, D_MODEL),
        u_p3[None, :, seq - (CONF_K - 1):],
        jnp.concatenate([conf_buf[:, 1:], u_s[:, None]], axis=1)[None],
        lx_p3[None, :, seq - (LRU_K - 1):],
        jnp.concatenate([lru_buf[:, 1:], lx_s[:, None]], axis=1)[None],
        hl_p.reshape(1, batch, D_LRU),
        hn_s[None],
    )
```

```python
import functools

import jax
import jax.numpy as jnp
from jax import lax
from jax.experimental import pallas as pl
from jax.experimental.pallas import tpu as pltpu

D_MODEL = 2048
D_CONF = 1024
D_LRU = 1024
N_HEADS = 8
HEAD = 128
CONF_K = 31
LRU_K = 4
LRU_C = 8.0
N_GROUPS = 4
EPG = 8
N_EXPERTS = 32
D_EXPERT = 512
D_PLE = 256
EPS = 1e-6

LANES = 128
SUBLANES = 8
VMEM_LIMIT = 56 * 1024 * 1024

TM = 512
TM_E = 256
CONF_TC = 64
LRU_TC = 256
LRU_PITCH = LRU_TC + SUBLANES
TM_PLE = 256

_BF16 = jnp.bfloat16
_F32 = jnp.float32


def _params(*sem):
    return pltpu.CompilerParams(dimension_semantics=sem, vmem_limit_bytes=VMEM_LIMIT)


def _const_spec(shape):
    nd = len(shape)
    return pl.BlockSpec(shape, lambda *_: (0,) * nd, pipeline_mode=pl.Buffered(1))


def _rms(x, g):
    return x * lax.rsqrt(jnp.mean(x * x, axis=-1, keepdims=True) + EPS) * g


def _dot(a, b):
    return jnp.dot(a, b, preferred_element_type=_F32)


def _inproj_body(x_ref, g_ref, w_ref, b_ref, u_ref, lx_ref, lg_ref):
    xn = _rms(x_ref[...], g_ref[...]).astype(_BF16)

    def proj(k):
        sl = slice(k * D_CONF, (k + 1) * D_CONF)
        return _dot(xn, w_ref[:, sl]) + b_ref[:, sl]

    u_ref[...] = proj(0) * jax.nn.sigmoid(proj(1))
    lx_ref[...] = proj(2)
    lg_ref[...] = jax.nn.gelu(proj(3))


def _inproj(x, g, w, b, tm):
    n = x.shape[0]
    out = jax.ShapeDtypeStruct((n, D_CONF), _F32)
    tok = lambda width: pl.BlockSpec((tm, width), lambda i: (i, 0))
    return pl.pallas_call(
        _inproj_body,
        out_shape=(out, out, out),
        grid=(n // tm,),
        in_specs=[tok(D_MODEL), _const_spec((1, D_MODEL)), _const_spec((D_MODEL, 4 * D_CONF)),
                  _const_spec((1, 4 * D_CONF))],
        out_specs=(tok(D_CONF), tok(D_CONF), tok(D_CONF)),
        compiler_params=_params("arbitrary"),
        name="inproj",
    )(x, g, w, b)


def _layernorm_silu(conv, g, b):
    mu = jnp.mean(conv, axis=-1, keepdims=True)
    cen = conv - mu
    var = jnp.mean(cen * cen, axis=-1, keepdims=True)
    y = cen * lax.rsqrt(var + EPS) * g + b
    return y * jax.nn.sigmoid(y)


def _conf_body(u_ref, w_ref, b_ref, g_ref, lb_ref, c_ref, s_ref, *, seq):
    head = 4 * SUBLANES
    s_ref[0:head, :] = jnp.zeros((head, D_CONF), _F32)
    s_ref[head:head + seq, :] = u_ref[...]
    s_ref[head + seq:, :] = jnp.zeros((SUBLANES, D_CONF), _F32)
    tc = CONF_TC

    def chunk(ci, carry):
        t0 = pl.multiple_of(ci * tc, tc)
        win = s_ref[pl.ds(t0, tc + 5 * SUBLANES), :]
        out = None
        for r in range(SUBLANES):
            part = None
            for q in range(5):
                m = SUBLANES * q + r
                if 2 <= m <= CONF_K + 1:
                    term = win[SUBLANES * q:SUBLANES * q + tc + SUBLANES, :] * w_ref[m - 2:m - 1, :]
                    part = term if part is None else part + term
            shifted = part[r:r + tc, :]
            out = shifted if out is None else out + shifted
        c = _layernorm_silu(out + b_ref[...], g_ref[...], lb_ref[...])
        c_ref[pl.ds(t0, tc), :] = c.astype(_BF16)
        return carry

    lax.fori_loop(0, seq // tc, chunk, 0)


def _conf_prompt(u, w, b, g, lb, batch, seq):
    return pl.pallas_call(
        functools.partial(_conf_body, seq=seq),
        out_shape=jax.ShapeDtypeStruct((batch * seq, D_CONF), _BF16),
        grid=(batch,),
        in_specs=[pl.BlockSpec((seq, D_CONF), lambda i: (i, 0)), _const_spec((CONF_K, D_CONF)),
                  _const_spec((1, D_CONF)), _const_spec((1, D_CONF)), _const_spec((1, D_CONF))],
        out_specs=pl.BlockSpec((seq, D_CONF), lambda i: (i, 0)),
        scratch_shapes=[pltpu.VMEM((seq + 5 * SUBLANES, D_CONF), _F32)],
        compiler_params=_params("arbitrary"),
        name="conf_prompt",
    )(u, w, b, g, lb)


def _conf_step_body(buf_ref, u_ref, w_ref, b_ref, g_ref, lb_ref, c_ref):
    w = w_ref[...]
    conv = jnp.sum(buf_ref[...] * w[None, :CONF_K - 1, :], axis=1) + u_ref[...] * w[CONF_K - 1:CONF_K, :]
    c_ref[...] = _layernorm_silu(conv + b_ref[...], g_ref[...], lb_ref[...]).astype(_BF16)


def _conf_step(buf, u, w, b, g, lb):
    nb = buf.shape[0]
    bb = 16
    return pl.pallas_call(
        _conf_step_body,
        out_shape=jax.ShapeDtypeStruct((nb, D_CONF), _BF16),
        grid=(nb // bb,),
        in_specs=[pl.BlockSpec((bb, CONF_K - 1, D_CONF), lambda i: (i, 0, 0)),
                  pl.BlockSpec((bb, D_CONF), lambda i: (i, 0)), _const_spec((CONF_K, D_CONF)),
                  _const_spec((1, D_CONF)), _const_spec((1, D_CONF)), _const_spec((1, D_CONF))],
        out_specs=pl.BlockSpec((bb, D_CONF), lambda i: (i, 0)),
        compiler_params=_params("arbitrary"),
        name="conf_step",
    )(buf, u, w, b, g, lb)


def _lru_gates(lx, wg_ref, bg_ref, lam_ref, head):
    sl = slice(head * HEAD, (head + 1) * HEAD)
    lxh = lx[:, sl]
    z = _dot(lxh.astype(_BF16), wg_ref[head]) + bg_ref[head:head + 1, :]
    r = jax.nn.sigmoid(z[:, :HEAD])
    i = jax.nn.sigmoid(z[:, HEAD:])
    log_a = -LRU_C * r * jax.nn.softplus(-lam_ref[:, sl])
    a = jnp.exp(log_a)
    return a, jnp.sqrt(-jnp.tanh(log_a) * (a * a + 1.0)) * i * lxh


def _lru_body(lx_ref, lg_ref, cw_ref, cb_ref, wg_ref, bg_ref, lam_ref, r_ref, hl_ref,
              halo_ref, h_ref, a_ref, u_ref, hs_ref):
    j = pl.program_id(1)
    tc = LRU_TC

    @pl.when(j == 0)
    def _():
        halo_ref[...] = jnp.zeros_like(halo_ref)
        h_ref[...] = jnp.zeros_like(h_ref)

    cur = lx_ref[...]
    ext = jnp.concatenate([halo_ref[...], cur], axis=0)
    lx = cb_ref[...] + cw_ref[LRU_K - 1:LRU_K, :] * cur
    for back in range(1, LRU_K):
        lx = lx + cw_ref[LRU_K - 1 - back:LRU_K - back, :] * ext[SUBLANES - back:SUBLANES - back + tc, :]
    halo_ref[...] = cur[tc - SUBLANES:, :]

    for head in range(N_HEADS):
        a, u = _lru_gates(lx, wg_ref, bg_ref, lam_ref, head)
        a_ref[head * LRU_PITCH:head * LRU_PITCH + tc, :] = a
        u_ref[head * LRU_PITCH:head * LRU_PITCH + tc, :] = u

    def step(t, h):
        rows = pl.ds(t, N_HEADS, stride=LRU_PITCH)
        h = a_ref[rows, :] * h + u_ref[rows, :]
        hs_ref[rows, :] = h
        return h

    h_ref[...] = lax.fori_loop(0, tc, step, h_ref[...], unroll=8)

    for head in range(N_HEADS):
        sl = slice(head * HEAD, (head + 1) * HEAD)
        hs = hs_ref[head * LRU_PITCH:head * LRU_PITCH + tc, :]
        r_ref[:, sl] = (hs * lg_ref[:, sl]).astype(_BF16)

    @pl.when(j == pl.num_programs(1) - 1)
    def _():
        hl_ref[0] = h_ref[...]


def _lru_prompt(lx, lg, cw, cb, wg, bg, lam, batch, seq):
    nchunk = seq // LRU_TC
    tok = pl.BlockSpec((LRU_TC, D_LRU), lambda b, j: (b * nchunk + j, 0))
    pitch_rows = N_HEADS * LRU_PITCH
    return pl.pallas_call(
        _lru_body,
        out_shape=(jax.ShapeDtypeStruct((batch * seq, D_LRU), _BF16),
                   jax.ShapeDtypeStruct((batch, N_HEADS, HEAD), _F32)),
        grid=(batch, nchunk),
        in_specs=[tok, tok, _const_spec((LRU_K, D_LRU)), _const_spec((1, D_LRU)),
                  _const_spec((N_HEADS, HEAD, 2 * HEAD)), _const_spec((N_HEADS, 2 * HEAD)),
                  _const_spec((1, D_LRU))],
        out_specs=(tok, pl.BlockSpec((1, N_HEADS, HEAD), lambda b, j: (b, 0, 0))),
        scratch_shapes=[pltpu.VMEM((SUBLANES, D_LRU), _F32), pltpu.VMEM((N_HEADS, HEAD), _F32),
                        pltpu.VMEM((pitch_rows, HEAD), _F32), pltpu.VMEM((pitch_rows, HEAD), _F32),
                        pltpu.VMEM((pitch_rows, HEAD), _F32)],
        compiler_params=_params("arbitrary", "arbitrary"),
        name="lru_prompt",
    )(lx, lg, cw, cb, wg, bg, lam)


def _lru_step_body(b0_ref, b1_ref, b2_ref, lx_ref, h0_ref, lg_ref, cw_ref, cb_ref, wg_ref, bg_ref, lam_ref,
                   r_ref, hn_ref):
    lx = (cb_ref[...] + cw_ref[0:1, :] * b0_ref[...] + cw_ref[1:2, :] * b1_ref[...]
          + cw_ref[2:3, :] * b2_ref[...] + cw_ref[3:4, :] * lx_ref[...])
    for head in range(N_HEADS):
        sl = slice(head * HEAD, (head + 1) * HEAD)
        a, u = _lru_gates(lx, wg_ref, bg_ref, lam_ref, head)
        h = a * h0_ref[:, sl] + u
        hn_ref[:, sl] = h
        r_ref[:, sl] = (h * lg_ref[:, sl]).astype(_BF16)


def _lru_step(b0, b1, b2, lx, h0, lg, cw, cb, wg, bg, lam):
    nb = lx.shape[0]
    full = _const_spec((nb, D_LRU))
    return pl.pallas_call(
        _lru_step_body,
        out_shape=(jax.ShapeDtypeStruct((nb, D_LRU), _BF16), jax.ShapeDtypeStruct((nb, D_LRU), _F32)),
        grid=(1,),
        in_specs=[full] * 6 + [_const_spec((LRU_K, D_LRU)), _const_spec((1, D_LRU)),
                               _const_spec((N_HEADS, HEAD, 2 * HEAD)), _const_spec((N_HEADS, 2 * HEAD)),
                               _const_spec((1, D_LRU))],
        out_specs=(pl.BlockSpec((nb, D_LRU), lambda i: (0, 0)), pl.BlockSpec((nb, D_LRU), lambda i: (0, 0))),
        compiler_params=_params("arbitrary"),
        name="lru_step",
    )(b0, b1, b2, lx, h0, lg, cw, cb, wg, bg, lam)


R_E1, R_E2, R_W1, R_W2, R_RANK1, R_RANK2 = range(6)


def _route_tile(logits, seen):
    m = logits.shape[0]
    lane = lax.broadcasted_iota(jnp.int32, (m, LANES), 1).astype(_F32)
    neg = jnp.float32(-jnp.inf)

    def first_max(vals):
        vmax = jnp.max(vals, axis=1, keepdims=True)
        return vmax, jnp.min(jnp.where(vals == vmax, lane, float(LANES)), axis=1, keepdims=True)

    in_groups = lane < N_GROUPS
    g_max, g_idx = first_max(jnp.where(in_groups, logits, neg))
    g_den = jnp.sum(jnp.where(in_groups, jnp.exp(logits - g_max), 0.0), axis=1, keepdims=True)
    lo = N_GROUPS + EPG * g_idx
    e_logits = jnp.where((lane >= lo) & (lane < lo + EPG), logits, neg)
    v1, i1 = first_max(e_logits)
    v2, i2 = first_max(jnp.where(lane == i1, neg, e_logits))
    t = jnp.exp(v2 - v1)
    w1 = 1.0 / (g_den * (1.0 + t))
    w2 = w1 * t
    e1 = i1 - N_GROUPS
    e2 = i2 - N_GROUPS

    is1 = lane == e1
    is2 = lane == e2
    onehot = jnp.where(is1 | is2, 1.0, 0.0)
    earlier = (lax.broadcasted_iota(jnp.int32, (m, m), 1) < lax.broadcasted_iota(jnp.int32, (m, m), 0))
    before = seen + _dot(jnp.where(earlier, 1.0, 0.0).astype(_BF16), onehot.astype(_BF16))
    rank1 = jnp.sum(jnp.where(is1, before, 0.0), axis=1, keepdims=True)
    rank2 = jnp.sum(jnp.where(is2, before, 0.0), axis=1, keepdims=True)

    rec = jnp.zeros((m, LANES), _F32)
    for pos, val in ((R_E1, e1), (R_E2, e2), (R_W1, w1), (R_W2, w2), (R_RANK1, rank1), (R_RANK2, rank2)):
        rec = jnp.where(lane == pos, val, rec)
    return rec, seen + jnp.sum(onehot, axis=0, keepdims=True)


def _outproj_body(cp_ref, rp_ref, xp_ref, cs_ref, rs_ref, xs_ref, wc_ref, wr_ref, g_ref, wrt_ref, brt_ref,
                  h1_ref, xn_ref, rec_ref, cnt_ref, seen_ref, *, n_ptiles, nb):
    i = pl.program_id(0)

    @pl.when(i == 0)
    def _():
        seen_ref[...] = jnp.zeros_like(seen_ref)

    def mix(c, r, x):
        h1 = x + _dot(c, wc_ref[...]) + _dot(r, wr_ref[...])
        xn = _rms(h1, g_ref[...])
        logits = _dot(xn.astype(_BF16), wrt_ref[...]) + brt_ref[...]
        rec, seen = _route_tile(logits, seen_ref[...])
        seen_ref[...] = seen
        return h1, xn, rec

    @pl.when(i < n_ptiles)
    def _():
        h1_ref[...], xn_ref[...], rec_ref[...] = mix(cp_ref[...], rp_ref[...], xp_ref[...])

    @pl.when(i == n_ptiles)
    def _():
        for ref, val in zip((h1_ref, xn_ref, rec_ref), mix(cs_ref[...], rs_ref[...], xs_ref[...])):
            ref[:nb, :] = val
            ref[nb:, :] = jnp.zeros((ref.shape[0] - nb, ref.shape[1]), _F32)

    cnt_ref[...] = jnp.broadcast_to(seen_ref[...], cnt_ref.shape)


def _outproj(cp, rp, xp, cs, rs, xs, wc, wr, g, wrt, brt, tm):
    n_ptiles = xp.shape[0] // tm
    nb = xs.shape[0]
    rows = (n_ptiles + 1) * tm
    last = n_ptiles - 1
    tok = lambda width: pl.BlockSpec((tm, width), lambda i: (jnp.minimum(i, last), 0))
    out_tok = lambda width: pl.BlockSpec((tm, width), lambda i: (i, 0))
    return pl.pallas_call(
        functools.partial(_outproj_body, n_ptiles=n_ptiles, nb=nb),
        out_shape=(jax.ShapeDtypeStruct((rows, D_MODEL), _F32), jax.ShapeDtypeStruct((rows, D_MODEL), _F32),
                   jax.ShapeDtypeStruct((rows, LANES), _F32), jax.ShapeDtypeStruct((SUBLANES, LANES), _F32)),
        grid=(n_ptiles + 1,),
        in_specs=[tok(D_CONF), tok(D_LRU), tok(D_MODEL), _const_spec((nb, D_CONF)), _const_spec((nb, D_LRU)),
                  _const_spec((nb, D_MODEL)), _const_spec((D_CONF, D_MODEL)), _const_spec((D_LRU, D_MODEL)),
                  _const_spec((1, D_MODEL)), _const_spec((D_MODEL, LANES)), _const_spec((1, LANES))],
        out_specs=(out_tok(D_MODEL), out_tok(D_MODEL), out_tok(LANES),
                   pl.BlockSpec((SUBLANES, LANES), lambda i: (0, 0))),
        scratch_shapes=[pltpu.VMEM((1, LANES), _F32)],
        compiler_params=_params("arbitrary"),
        name="outproj",
    )(cp, rp, xp, cs, rs, xs, wc, wr, g, wrt, brt)


def _moe_body(te_ref, nu_ref, tbl_ref, xn_hbm, wg_ref, wu_ref, wd_ref, y_ref, xbuf, wg_b, wu_b, wd_b, sem):
    j = pl.program_id(0)
    n_used = nu_ref[0]
    cur = lax.rem(j, 2)

    def start_gather(tile, buf):
        for r in range(TM_E):
            tok = tbl_ref[tile * TM_E + r]
            pltpu.make_async_copy(xn_hbm.at[pl.ds(tok, 1), :], xbuf.at[buf, pl.ds(r, 1), :], sem.at[buf]).start()

    def wait_gather(buf):
        pltpu.make_async_copy(xn_hbm.at[pl.ds(0, TM_E), :], xbuf.at[buf], sem.at[buf]).wait()

    def experts():
        x = xbuf[cur].astype(_BF16)
        g = _dot(x, wg_b[...])
        u = _dot(x, wu_b[...])
        hid = g * jax.nn.sigmoid(g) * u
        y_ref[...] = _dot(hid.astype(_BF16), wd_b[...])

    @pl.when(j == 0)
    def _():
        start_gather(0, 0)

    @pl.when(j < n_used)
    def _():
        wait_gather(cur)

        @pl.when((j == 0) | (te_ref[j] != te_ref[jnp.maximum(j - 1, 0)]))
        def _():
            wg_b[...] = wg_ref[0].astype(_BF16)
            wu_b[...] = wu_ref[0].astype(_BF16)
            wd_b[...] = wd_ref[0].astype(_BF16)

    @pl.when(j + 1 < n_used)
    def _():
        experts()
        start_gather(j + 1, 1 - cur)

    @pl.when(j + 1 == n_used)
    def _():
        experts()

    @pl.when(j >= n_used)
    def _():
        y_ref[...] = jnp.zeros_like(y_ref)


def _moe(tile_expert, n_used, table, xn, w_gate, w_up, w_down):
    n_tiles = tile_expert.shape[0]
    by_expert = lambda shape: pl.BlockSpec(shape, lambda j, te, nu, tbl: (te[j], 0, 0))
    grid_spec = pltpu.PrefetchScalarGridSpec(
        num_scalar_prefetch=3,
        grid=(n_tiles,),
        in_specs=[pl.BlockSpec(memory_space=pl.ANY), by_expert((1, D_MODEL, D_EXPERT)),
                  by_expert((1, D_MODEL, D_EXPERT)), by_expert((1, D_EXPERT, D_MODEL))],
        out_specs=pl.BlockSpec((TM_E, D_MODEL), lambda j, te, nu, tbl: (j, 0)),
        scratch_shapes=[pltpu.VMEM((2, TM_E, D_MODEL), _F32),
                        pltpu.VMEM((D_MODEL, D_EXPERT), _BF16), pltpu.VMEM((D_MODEL, D_EXPERT), _BF16),
                        pltpu.VMEM((D_EXPERT, D_MODEL), _BF16), pltpu.SemaphoreType.DMA((2,))],
    )
    return pl.pallas_call(
        _moe_body,
        out_shape=jax.ShapeDtypeStruct((n_tiles * TM_E, D_MODEL), _F32),
        grid_spec=grid_spec,
        compiler_params=_params("arbitrary"),
        name="moe",
    )(tile_expert, n_used, table, xn, w_gate, w_up, w_down)


def _ple_body(slot_ref, h1_ref, rec_ref, p_ref, g_ref, wpg_ref, bpg_ref, wple_ref, fg_ref, ys_hbm, o_ref,
              ybuf, sem, *, tm):
    i = pl.program_id(0)
    cur = lax.rem(i, 2)

    def start_gather(tile, buf):
        for r in range(tm):
            for k in range(2):
                s = slot_ref[(tile * tm + r) * 2 + k]
                pltpu.make_async_copy(ys_hbm.at[pl.ds(s, 1), :], ybuf.at[buf, k, pl.ds(r, 1), :], sem.at[buf]).start()

    def wait_gather(buf):
        for k in range(2):
            pltpu.make_async_copy(ys_hbm.at[pl.ds(0, tm), :], ybuf.at[buf, k], sem.at[buf]).wait()

    def finish():
        rec = rec_ref[...]
        h2 = h1_ref[...] + rec[:, R_W1:R_W1 + 1] * ybuf[cur, 0] + rec[:, R_W2:R_W2 + 1] * ybuf[cur, 1]
        hn = _rms(h2, g_ref[...]).astype(_BF16)
        gate = jax.nn.sigmoid(_dot(hn, wpg_ref[...]) + bpg_ref[...])
        pe = _dot(p_ref[...].astype(_BF16), wple_ref[...])
        o_ref[...] = _rms(h2 + gate * pe, fg_ref[...])

    @pl.when(i == 0)
    def _():
        start_gather(0, 0)

    wait_gather(cur)

    @pl.when(i + 1 < pl.num_programs(0))
    def _():
        finish()
        start_gather(i + 1, 1 - cur)

    @pl.when(i + 1 == pl.num_programs(0))
    def _():
        finish()


def _ple(h1, ys, slots, rec, p, g, wpg, bpg, wple, fg, tm, row0):
    n = p.shape[0]
    off = row0 // tm
    tok = lambda width: pl.BlockSpec((tm, width), lambda i, s: (i + off, 0))
    const = lambda shape: pl.BlockSpec(shape, lambda i, s: (0,) * len(shape), pipeline_mode=pl.Buffered(1))
    grid_spec = pltpu.PrefetchScalarGridSpec(
        num_scalar_prefetch=1,
        grid=(n // tm,),
        in_specs=[tok(D_MODEL), tok(LANES), pl.BlockSpec((tm, D_PLE), lambda i, s: (i, 0)),
                  const((1, D_MODEL)), const((D_MODEL, D_MODEL)), const((1, D_MODEL)),
                  const((D_PLE, D_MODEL)), const((1, D_MODEL)), pl.BlockSpec(memory_space=pl.ANY)],
        out_specs=pl.BlockSpec((tm, D_MODEL), lambda i, s: (i, 0)),
        scratch_shapes=[pltpu.VMEM((2, 2, tm, D_MODEL), _F32), pltpu.SemaphoreType.DMA((2,))],
    )
    return pl.pallas_call(
        functools.partial(_ple_body, tm=tm),
        out_shape=jax.ShapeDtypeStruct((n, D_MODEL), _F32),
        grid_spec=grid_spec,
        compiler_params=_params("arbitrary"),
        name="ple",
    )(slots, h1, rec, p, g, wpg, bpg, wple, fg, ys)


def _schedule(rec, counts_row, n_all, n_tiles):
    ids = jnp.arange(N_EXPERTS, dtype=jnp.int32)
    expert = rec[:n_all, R_E1:R_E2 + 1].astype(jnp.int32)
    rank = rec[:n_all, R_RANK1:R_RANK2 + 1].astype(jnp.int32)
    counts = counts_row[0, :N_EXPERTS].astype(jnp.int32)
    tiles_per = (counts + TM_E - 1) // TM_E
    tiles_end = jnp.cumsum(tiles_per)
    n_used = tiles_end[-1]
    slot = jnp.sum(jnp.where(expert[..., None] == ids, (tiles_end - tiles_per) * TM_E, 0), axis=-1) + rank
    token = jnp.broadcast_to(jnp.arange(n_all, dtype=jnp.int32)[:, None], (n_all, 2))
    table = jnp.zeros((n_tiles * TM_E,), jnp.int32).at[slot.reshape(-1)].set(token.reshape(-1), unique_indices=True)
    tile_ids = jnp.arange(n_tiles, dtype=jnp.int32)
    te = jnp.minimum(jnp.sum((tiles_end[None, :] <= tile_ids[:, None]).astype(jnp.int32), axis=1), N_EXPERTS - 1)
    last_expert = jnp.sum(jnp.where(tile_ids == n_used - 1, te, 0))
    te = jnp.where(tile_ids < n_used, te, last_expert).astype(jnp.int32)
    return te, n_used.astype(jnp.int32).reshape(1), table, slot.reshape(-1)


def kernel(x_prompt, x_sample, state_conf_conv, state_lru_conv, state_lru_h, p_prompt, p_sample, norm1_g, w_in, b_in, conf_dw_w, conf_dw_b, conf_ln_g, conf_ln_b, lru_conv_w, lru_conv_b, lru_wa, lru_ba, lru_wx, lru_bx, lru_lambda, w_out, norm2_g, w_grp, b_grp, w_rt, b_rt, w_gate, w_up, w_down, ple_norm_g, w_ple, w_pg, b_pg, final_g):
    batch, seq, _ = x_prompt.shape
    nb = x_sample.shape[0]
    n_p = batch * seq
    n_all = n_p + nb
    layer = 0

    row = lambda v: v.reshape(1, -1)
    w_in_b = w_in[layer].astype(_BF16)
    wc_b = w_out[layer, :D_CONF].astype(_BF16)
    wr_b = w_out[layer, D_CONF:].astype(_BF16)
    w_router = jnp.concatenate(
        [w_grp[layer], w_rt[layer], jnp.zeros((D_MODEL, LANES - N_GROUPS - N_EXPERTS), _F32)], axis=1).astype(_BF16)
    b_router = jnp.concatenate([b_grp[layer], b_rt[layer], jnp.zeros((LANES - N_GROUPS - N_EXPERTS,), _F32)])
    w_gates = jnp.concatenate([lru_wa[layer], lru_wx[layer]], axis=-1).astype(_BF16)
    b_gates = jnp.concatenate([lru_ba[layer], lru_bx[layer]], axis=-1)
    w_pg_b = w_pg[layer].astype(_BF16)
    w_ple_b = w_ple[layer].astype(_BF16)
    conf_args = (conf_dw_w[layer], row(conf_dw_b[layer]), row(conf_ln_g[layer]), row(conf_ln_b[layer]))
    lru_args = (lru_conv_w[layer], row(lru_conv_b[layer]), w_gates, b_gates, row(lru_lambda[layer]))
    out_args = (wc_b, wr_b, row(norm2_g[layer]), w_router, row(b_router))
    ple_args = (row(ple_norm_g[layer]), w_pg_b, row(b_pg[layer]), w_ple_b, row(final_g))

    xp = x_prompt.reshape(n_p, D_MODEL)
    u_p, lx_p, lg_p = _inproj(xp, row(norm1_g[layer]), w_in_b, row(b_in[layer]), TM)
    c_p = _conf_prompt(u_p, *conf_args, batch, seq)
    r_p, hl_p = _lru_prompt(lx_p, lg_p, *lru_args, batch, seq)

    xs = x_sample.reshape(nb, D_MODEL)
    conf_buf = state_conf_conv[layer]
    lru_buf = state_lru_conv[layer]
    u_s, lx_s, lg_s = _inproj(xs, row(norm1_g[layer]), w_in_b, row(b_in[layer]), nb)
    c_s = _conf_step(conf_buf, u_s, *conf_args)
    r_s, hn_s = _lru_step(lru_buf[:, 0], lru_buf[:, 1], lru_buf[:, 2], lx_s, state_lru_h[layer], lg_s, *lru_args)
    h1, xn, rec, counts = _outproj(c_p, r_p, xp, c_s, r_s, xs, *out_args, TM)

    n_tiles = (2 * n_all + TM_E - 1) // TM_E + N_EXPERTS
    tile_expert, n_used, table, slots = _schedule(rec, counts, n_all, n_tiles)
    ys = _moe(tile_expert, n_used, table, xn, w_gate[layer], w_up[layer], w_down[layer])

    y_p = _ple(h1, ys, slots[:2 * n_p], rec, p_prompt[layer].reshape(n_p, D_PLE), *ple_args, TM_PLE, 0)
    y_s = _ple(h1, ys, slots[2 * n_p:], rec, p_sample[layer].reshape(nb, D_PLE), *ple_args, nb, n_p)

    u_p3 = u_p.reshape(batch, seq, D_CONF)
    lx_p3 = lx_p.reshape(batch, seq, D_LRU)
    return (
        y_p.reshape(batch, seq, D_MODEL),
        y_s.reshape(nb, 1, D_MODEL),
        u_p3[None, :, seq - (CONF_K - 1):],
        jnp.concatenate([conf_buf[:, 1:], u_s[:, None]], axis=1)[None],
        lx_p3[None, :, seq - (LRU_K - 1):],
        jnp.concatenate([lru_buf[:, 1:], lx_s[:, None]], axis=1)[None],
        hl_p.reshape(1, batch, D_LRU),
        hn_s[None],
    )
```

```python
import functools

import jax
import jax.numpy as jnp
from jax import lax
from jax.experimental import pallas as pl
from jax.experimental.pallas import tpu as pltpu

D_MODEL = 2048
D_CONF = 1024
D_LRU = 1024
N_HEADS = 8
HEAD = 128
CONF_K = 31
LRU_K = 4
LRU_C = 8.0
N_GROUPS = 4
EPG = 8
N_EXPERTS = 32
D_EXPERT = 512
D_PLE = 256
EPS = 1e-6

LANES = 128
SUBLANES = 8
VMEM_LIMIT = 56 * 1024 * 1024

TM = 512
TM_E = 256
CONF_TC = 64
LRU_TC = 256
LRU_PITCH = LRU_TC + SUBLANES
TM_PLE = 256
TOK_ROWS = D_MODEL // LANES
TOK_PITCH = TOK_ROWS + SUBLANES

_BF16 = jnp.bfloat16
_F32 = jnp.float32


def _params(*sem):
    return pltpu.CompilerParams(dimension_semantics=sem, vmem_limit_bytes=VMEM_LIMIT)


def _const_spec(shape):
    nd = len(shape)
    return pl.BlockSpec(shape, lambda *_: (0,) * nd, pipeline_mode=pl.Buffered(1))


def _store_token_major(ref, val, pitch=TOK_ROWS):
    for tb in range(val.shape[0] // SUBLANES):
        for s in range(TOK_ROWS):
            ref[pl.ds(tb * SUBLANES * pitch + s, SUBLANES, stride=pitch), :] = (
                val[tb * SUBLANES:(tb + 1) * SUBLANES, s * LANES:(s + 1) * LANES])


def _load_token_major(load, m, pitch):
    slabs = [jnp.concatenate([load(pl.ds(tb * SUBLANES * pitch + s, SUBLANES, stride=pitch))
                              for tb in range(m // SUBLANES)], axis=0) for s in range(TOK_ROWS)]
    return jnp.concatenate(slabs, axis=1)


def _rms(x, g):
    return x * lax.rsqrt(jnp.mean(x * x, axis=-1, keepdims=True) + EPS) * g


def _dot(a, b):
    return jnp.dot(a, b, preferred_element_type=_F32)


def _inproj_body(x_ref, g_ref, w_ref, b_ref, u_ref, lx_ref, lg_ref):
    xn = _rms(x_ref[...], g_ref[...]).astype(_BF16)

    def proj(k):
        sl = slice(k * D_CONF, (k + 1) * D_CONF)
        return _dot(xn, w_ref[:, sl]) + b_ref[:, sl]

    u_ref[...] = proj(0) * jax.nn.sigmoid(proj(1))
    lx_ref[...] = proj(2)
    lg_ref[...] = jax.nn.gelu(proj(3))


def _inproj(x, g, w, b, tm):
    n = x.shape[0]
    out = jax.ShapeDtypeStruct((n, D_CONF), _F32)
    tok = lambda width: pl.BlockSpec((tm, width), lambda i: (i, 0))
    return pl.pallas_call(
        _inproj_body,
        out_shape=(out, out, out),
        grid=(n // tm,),
        in_specs=[tok(D_MODEL), _const_spec((1, D_MODEL)), _const_spec((D_MODEL, 4 * D_CONF)),
                  _const_spec((1, 4 * D_CONF))],
        out_specs=(tok(D_CONF), tok(D_CONF), tok(D_CONF)),
        compiler_params=_params("arbitrary"),
        name="inproj",
    )(x, g, w, b)


def _layernorm_silu(conv, g, b):
    mu = jnp.mean(conv, axis=-1, keepdims=True)
    cen = conv - mu
    var = jnp.mean(cen * cen, axis=-1, keepdims=True)
    y = cen * lax.rsqrt(var + EPS) * g + b
    return y * jax.nn.sigmoid(y)


def _conf_body(u_ref, w_ref, b_ref, g_ref, lb_ref, c_ref, s_ref, *, seq):
    head = 4 * SUBLANES
    s_ref[0:head, :] = jnp.zeros((head, D_CONF), _F32)
    s_ref[head:head + seq, :] = u_ref[...]
    s_ref[head + seq:, :] = jnp.zeros((SUBLANES, D_CONF), _F32)
    tc = CONF_TC

    def chunk(ci, carry):
        t0 = pl.multiple_of(ci * tc, tc)
        win = s_ref[pl.ds(t0, tc + 5 * SUBLANES), :]
        out = None
        for r in range(SUBLANES):
            part = None
            for q in range(5):
                m = SUBLANES * q + r
                if 2 <= m <= CONF_K + 1:
                    term = win[SUBLANES * q:SUBLANES * q + tc + SUBLANES, :] * w_ref[m - 2:m - 1, :]
                    part = term if part is None else part + term
            shifted = part[r:r + tc, :]
            out = shifted if out is None else out + shifted
        c = _layernorm_silu(out + b_ref[...], g_ref[...], lb_ref[...])
        c_ref[pl.ds(t0, tc), :] = c.astype(_BF16)
        return carry

    lax.fori_loop(0, seq // tc, chunk, 0)


def _conf_prompt(u, w, b, g, lb, batch, seq):
    return pl.pallas_call(
        functools.partial(_conf_body, seq=seq),
        out_shape=jax.ShapeDtypeStruct((batch * seq, D_CONF), _BF16),
        grid=(batch,),
        in_specs=[pl.BlockSpec((seq, D_CONF), lambda i: (i, 0)), _const_spec((CONF_K, D_CONF)),
                  _const_spec((1, D_CONF)), _const_spec((1, D_CONF)), _const_spec((1, D_CONF))],
        out_specs=pl.BlockSpec((seq, D_CONF), lambda i: (i, 0)),
        scratch_shapes=[pltpu.VMEM((seq + 5 * SUBLANES, D_CONF), _F32)],
        compiler_params=_params("arbitrary"),
        name="conf_prompt",
    )(u, w, b, g, lb)


def _conf_step_body(buf_ref, u_ref, w_ref, b_ref, g_ref, lb_ref, c_ref):
    w = w_ref[...]
    conv = jnp.sum(buf_ref[...] * w[None, :CONF_K - 1, :], axis=1) + u_ref[...] * w[CONF_K - 1:CONF_K, :]
    c_ref[...] = _layernorm_silu(conv + b_ref[...], g_ref[...], lb_ref[...]).astype(_BF16)


def _conf_step(buf, u, w, b, g, lb):
    nb = buf.shape[0]
    bb = 16
    return pl.pallas_call(
        _conf_step_body,
        out_shape=jax.ShapeDtypeStruct((nb, D_CONF), _BF16),
        grid=(nb // bb,),
        in_specs=[pl.BlockSpec((bb, CONF_K - 1, D_CONF), lambda i: (i, 0, 0)),
                  pl.BlockSpec((bb, D_CONF), lambda i: (i, 0)), _const_spec((CONF_K, D_CONF)),
                  _const_spec((1, D_CONF)), _const_spec((1, D_CONF)), _const_spec((1, D_CONF))],
        out_specs=pl.BlockSpec((bb, D_CONF), lambda i: (i, 0)),
        compiler_params=_params("arbitrary"),
        name="conf_step",
    )(buf, u, w, b, g, lb)


def _lru_gates(lx, wg_ref, bg_ref, lam_ref, head):
    sl = slice(head * HEAD, (head + 1) * HEAD)
    lxh = lx[:, sl]
    z = _dot(lxh.astype(_BF16), wg_ref[head]) + bg_ref[head:head + 1, :]
    r = jax.nn.sigmoid(z[:, :HEAD])
    i = jax.nn.sigmoid(z[:, HEAD:])
    log_a = -LRU_C * r * jax.nn.softplus(-lam_ref[:, sl])
    a = jnp.exp(log_a)
    return a, jnp.sqrt(-jnp.tanh(log_a) * (a * a + 1.0)) * i * lxh


def _lru_body(lx_ref, lg_ref, cw_ref, cb_ref, wg_ref, bg_ref, lam_ref, r_ref, hl_ref,
              halo_ref, h_ref, a_ref, u_ref, hs_ref):
    j = pl.program_id(1)
    tc = LRU_TC

    @pl.when(j == 0)
    def _():
        halo_ref[...] = jnp.zeros_like(halo_ref)
        h_ref[...] = jnp.zeros_like(h_ref)

    cur = lx_ref[...]
    ext = jnp.concatenate([halo_ref[...], cur], axis=0)
    lx = cb_ref[...] + cw_ref[LRU_K - 1:LRU_K, :] * cur
    for back in range(1, LRU_K):
        lx = lx + cw_ref[LRU_K - 1 - back:LRU_K - back, :] * ext[SUBLANES - back:SUBLANES - back + tc, :]
    halo_ref[...] = cur[tc - SUBLANES:, :]

    for head in range(N_HEADS):
        a, u = _lru_gates(lx, wg_ref, bg_ref, lam_ref, head)
        a_ref[head * LRU_PITCH:head * LRU_PITCH + tc, :] = a
        u_ref[head * LRU_PITCH:head * LRU_PITCH + tc, :] = u

    def step(t, h):
        rows = pl.ds(t, N_HEADS, stride=LRU_PITCH)
        h = a_ref[rows, :] * h + u_ref[rows, :]
        hs_ref[rows, :] = h
        return h

    h_ref[...] = lax.fori_loop(0, tc, step, h_ref[...], unroll=8)

    for head in range(N_HEADS):
        sl = slice(head * HEAD, (head + 1) * HEAD)
        hs = hs_ref[head * LRU_PITCH:head * LRU_PITCH + tc, :]
        r_ref[:, sl] = (hs * lg_ref[:, sl]).astype(_BF16)

    @pl.when(j == pl.num_programs(1) - 1)
    def _():
        hl_ref[0] = h_ref[...]


def _lru_prompt(lx, lg, cw, cb, wg, bg, lam, batch, seq):
    nchunk = seq // LRU_TC
    tok = pl.BlockSpec((LRU_TC, D_LRU), lambda b, j: (b * nchunk + j, 0))
    pitch_rows = N_HEADS * LRU_PITCH
    return pl.pallas_call(
        _lru_body,
        out_shape=(jax.ShapeDtypeStruct((batch * seq, D_LRU), _BF16),
                   jax.ShapeDtypeStruct((batch, N_HEADS, HEAD), _F32)),
        grid=(batch, nchunk),
        in_specs=[tok, tok, _const_spec((LRU_K, D_LRU)), _const_spec((1, D_LRU)),
                  _const_spec((N_HEADS, HEAD, 2 * HEAD)), _const_spec((N_HEADS, 2 * HEAD)),
                  _const_spec((1, D_LRU))],
        out_specs=(tok, pl.BlockSpec((1, N_HEADS, HEAD), lambda b, j: (b, 0, 0))),
        scratch_shapes=[pltpu.VMEM((SUBLANES, D_LRU), _F32), pltpu.VMEM((N_HEADS, HEAD), _F32),
                        pltpu.VMEM((pitch_rows, HEAD), _F32), pltpu.VMEM((pitch_rows, HEAD), _F32),
                        pltpu.VMEM((pitch_rows, HEAD), _F32)],
        compiler_params=_params("arbitrary", "arbitrary"),
        name="lru_prompt",
    )(lx, lg, cw, cb, wg, bg, lam)


def _lru_step_body(b0_ref, b1_ref, b2_ref, lx_ref, h0_ref, lg_ref, cw_ref, cb_ref, wg_ref, bg_ref, lam_ref,
                   r_ref, hn_ref):
    lx = (cb_ref[...] + cw_ref[0:1, :] * b0_ref[...] + cw_ref[1:2, :] * b1_ref[...]
          + cw_ref[2:3, :] * b2_ref[...] + cw_ref[3:4, :] * lx_ref[...])
    for head in range(N_HEADS):
        sl = slice(head * HEAD, (head + 1) * HEAD)
        a, u = _lru_gates(lx, wg_ref, bg_ref, lam_ref, head)
        h = a * h0_ref[:, sl] + u
        hn_ref[:, sl] = h
        r_ref[:, sl] = (h * lg_ref[:, sl]).astype(_BF16)


def _lru_step(b0, b1, b2, lx, h0, lg, cw, cb, wg, bg, lam):
    nb = lx.shape[0]
    full = _const_spec((nb, D_LRU))
    return pl.pallas_call(
        _lru_step_body,
        out_shape=(jax.ShapeDtypeStruct((nb, D_LRU), _BF16), jax.ShapeDtypeStruct((nb, D_LRU), _F32)),
        grid=(1,),
        in_specs=[full] * 6 + [_const_spec((LRU_K, D_LRU)), _const_spec((1, D_LRU)),
                               _const_spec((N_HEADS, HEAD, 2 * HEAD)), _const_spec((N_HEADS, 2 * HEAD)),
                               _const_spec((1, D_LRU))],
        out_specs=(pl.BlockSpec((nb, D_LRU), lambda i: (0, 0)), pl.BlockSpec((nb, D_LRU), lambda i: (0, 0))),
        compiler_params=_params("arbitrary"),
        name="lru_step",
    )(b0, b1, b2, lx, h0, lg, cw, cb, wg, bg, lam)


R_E1, R_E2, R_W1, R_W2, R_RANK1, R_RANK2 = range(6)


def _route_tile(logits, seen):
    m = logits.shape[0]
    lane = lax.broadcasted_iota(jnp.int32, (m, LANES), 1).astype(_F32)
    neg = jnp.float32(-jnp.inf)

    def first_max(vals):
        vmax = jnp.max(vals, axis=1, keepdims=True)
        return vmax, jnp.min(jnp.where(vals == vmax, lane, float(LANES)), axis=1, keepdims=True)

    in_groups = lane < N_GROUPS
    g_max, g_idx = first_max(jnp.where(in_groups, logits, neg))
    g_den = jnp.sum(jnp.where(in_groups, jnp.exp(logits - g_max), 0.0), axis=1, keepdims=True)
    lo = N_GROUPS + EPG * g_idx
    e_logits = jnp.where((lane >= lo) & (lane < lo + EPG), logits, neg)
    v1, i1 = first_max(e_logits)
    v2, i2 = first_max(jnp.where(lane == i1, neg, e_logits))
    t = jnp.exp(v2 - v1)
    w1 = 1.0 / (g_den * (1.0 + t))
    w2 = w1 * t
    e1 = i1 - N_GROUPS
    e2 = i2 - N_GROUPS

    is1 = lane == e1
    is2 = lane == e2
    onehot = jnp.where(is1 | is2, 1.0, 0.0)
    earlier = (lax.broadcasted_iota(jnp.int32, (m, m), 1) < lax.broadcasted_iota(jnp.int32, (m, m), 0))
    before = seen + _dot(jnp.where(earlier, 1.0, 0.0).astype(_BF16), onehot.astype(_BF16))
    rank1 = jnp.sum(jnp.where(is1, before, 0.0), axis=1, keepdims=True)
    rank2 = jnp.sum(jnp.where(is2, before, 0.0), axis=1, keepdims=True)

    rec = jnp.zeros((m, LANES), _F32)
    for pos, val in ((R_E1, e1), (R_E2, e2), (R_W1, w1), (R_W2, w2), (R_RANK1, rank1), (R_RANK2, rank2)):
        rec = jnp.where(lane == pos, val, rec)
    return rec, seen + jnp.sum(onehot, axis=0, keepdims=True)


def _outproj_body(cp_ref, rp_ref, xp_ref, cs_ref, rs_ref, xs_ref, wc_ref, wr_ref, g_ref, wrt_ref, brt_ref,
                  h1_ref, xn_ref, rec_ref, cnt_ref, seen_ref, *, n_ptiles, nb):
    i = pl.program_id(0)

    @pl.when(i == 0)
    def _():
        seen_ref[...] = jnp.zeros_like(seen_ref)

    def mix(c, r, x):
        h1 = x + _dot(c, wc_ref[...]) + _dot(r, wr_ref[...])
        xn = _rms(h1, g_ref[...])
        logits = _dot(xn.astype(_BF16), wrt_ref[...]) + brt_ref[...]
        rec, seen = _route_tile(logits, seen_ref[...])
        seen_ref[...] = seen
        return h1, xn, rec

    @pl.when(i < n_ptiles)
    def _():
        h1_ref[...], xn, rec_ref[...] = mix(cp_ref[...], rp_ref[...], xp_ref[...])
        _store_token_major(xn_ref, xn)

    @pl.when(i == n_ptiles)
    def _():
        h1, xn, rec = mix(cs_ref[...], rs_ref[...], xs_ref[...])
        h1_ref[:nb, :] = h1
        rec_ref[:nb, :] = rec
        _store_token_major(xn_ref, xn)
        for ref, done in ((h1_ref, nb), (rec_ref, nb), (xn_ref, nb * TOK_ROWS)):
            ref[done:, :] = jnp.zeros((ref.shape[0] - done, ref.shape[1]), _F32)

    cnt_ref[...] = jnp.broadcast_to(seen_ref[...], cnt_ref.shape)


def _outproj(cp, rp, xp, cs, rs, xs, wc, wr, g, wrt, brt, tm):
    n_ptiles = xp.shape[0] // tm
    nb = xs.shape[0]
    rows = (n_ptiles + 1) * tm
    last = n_ptiles - 1
    tok = lambda width: pl.BlockSpec((tm, width), lambda i: (jnp.minimum(i, last), 0))
    out_tok = lambda width: pl.BlockSpec((tm, width), lambda i: (i, 0))
    return pl.pallas_call(
        functools.partial(_outproj_body, n_ptiles=n_ptiles, nb=nb),
        out_shape=(jax.ShapeDtypeStruct((rows, D_MODEL), _F32), jax.ShapeDtypeStruct((rows * TOK_ROWS, LANES), _F32),
                   jax.ShapeDtypeStruct((rows, LANES), _F32), jax.ShapeDtypeStruct((SUBLANES, LANES), _F32)),
        grid=(n_ptiles + 1,),
        in_specs=[tok(D_CONF), tok(D_LRU), tok(D_MODEL), _const_spec((nb, D_CONF)), _const_spec((nb, D_LRU)),
                  _const_spec((nb, D_MODEL)), _const_spec((D_CONF, D_MODEL)), _const_spec((D_LRU, D_MODEL)),
                  _const_spec((1, D_MODEL)), _const_spec((D_MODEL, LANES)), _const_spec((1, LANES))],
        out_specs=(out_tok(D_MODEL), pl.BlockSpec((tm * TOK_ROWS, LANES), lambda i: (i, 0)), out_tok(LANES),
                   pl.BlockSpec((SUBLANES, LANES), lambda i: (0, 0))),
        scratch_shapes=[pltpu.VMEM((1, LANES), _F32)],
        compiler_params=_params("arbitrary"),
        name="outproj",
    )(cp, rp, xp, cs, rs, xs, wc, wr, g, wrt, brt)


def _moe_body(te_ref, nu_ref, tbl_ref, xn_hbm, wg_ref, wu_ref, wd_ref, y_ref, xbuf, wg_b, wu_b, wd_b, sem):
    j = pl.program_id(0)
    n_used = nu_ref[0]
    cur = lax.rem(j, 2)

    def start_gather(tile, buf):
        for r in range(TM_E):
            row0 = pl.multiple_of(tbl_ref[tile * TM_E + r] * TOK_ROWS, TOK_ROWS)
            pltpu.make_async_copy(xn_hbm.at[pl.ds(row0, TOK_ROWS), :],
                                  xbuf.at[buf, pl.ds(r * TOK_PITCH, TOK_ROWS), :], sem.at[buf]).start()

    def wait_gather(buf):
        rows = pl.ds(0, TM_E * TOK_ROWS)
        pltpu.make_async_copy(xn_hbm.at[rows, :], xbuf.at[buf, rows, :], sem.at[buf]).wait()

    def experts():
        x = _load_token_major(lambda rows: xbuf[cur, rows, :], TM_E, TOK_PITCH).astype(_BF16)
        g = _dot(x, wg_b[...])
        u = _dot(x, wu_b[...])
        hid = g * jax.nn.sigmoid(g) * u
        _store_token_major(y_ref, _dot(hid.astype(_BF16), wd_b[...]))

    @pl.when(j == 0)
    def _():
        start_gather(0, 0)

    @pl.when(j < n_used)
    def _():
        wait_gather(cur)

        @pl.when((j == 0) | (te_ref[j] != te_ref[jnp.maximum(j - 1, 0)]))
        def _():
            wg_b[...] = wg_ref[0].astype(_BF16)
            wu_b[...] = wu_ref[0].astype(_BF16)
            wd_b[...] = wd_ref[0].astype(_BF16)

    @pl.when(j + 1 < n_used)
    def _():
        experts()
        start_gather(j + 1, 1 - cur)

    @pl.when(j + 1 == n_used)
    def _():
        experts()

    @pl.when(j >= n_used)
    def _():
        y_ref[...] = jnp.zeros_like(y_ref)


def _moe(tile_expert, n_used, table, xn, w_gate, w_up, w_down):
    n_tiles = tile_expert.shape[0]
    by_expert = lambda shape: pl.BlockSpec(shape, lambda j, te, nu, tbl: (te[j], 0, 0))
    grid_spec = pltpu.PrefetchScalarGridSpec(
        num_scalar_prefetch=3,
        grid=(n_tiles,),
        in_specs=[pl.BlockSpec(memory_space=pl.ANY), by_expert((1, D_MODEL, D_EXPERT)),
                  by_expert((1, D_MODEL, D_EXPERT)), by_expert((1, D_EXPERT, D_MODEL))],
        out_specs=pl.BlockSpec((TM_E * TOK_ROWS, LANES), lambda j, te, nu, tbl: (j, 0)),
        scratch_shapes=[pltpu.VMEM((2, TM_E * TOK_PITCH, LANES), _F32),
                        pltpu.VMEM((D_MODEL, D_EXPERT), _BF16), pltpu.VMEM((D_MODEL, D_EXPERT), _BF16),
                        pltpu.VMEM((D_EXPERT, D_MODEL), _BF16), pltpu.SemaphoreType.DMA((2,))],
    )
    return pl.pallas_call(
        _moe_body,
        out_shape=jax.ShapeDtypeStruct((n_tiles * TM_E * TOK_ROWS, LANES), _F32),
        grid_spec=grid_spec,
        compiler_params=_params("arbitrary"),
        name="moe",
    )(tile_expert, n_used, table, xn, w_gate, w_up, w_down)


def _ple_body(slot_ref, h1_ref, rec_ref, p_ref, g_ref, wpg_ref, bpg_ref, wple_ref, fg_ref, ys_hbm, o_ref,
              ybuf, sem, *, tm):
    i = pl.program_id(0)
    cur = lax.rem(i, 2)

    def start_gather(tile, buf):
        for r in range(tm):
            for k in range(2):
                row0 = pl.multiple_of(slot_ref[(tile * tm + r) * 2 + k] * TOK_ROWS, TOK_ROWS)
                pltpu.make_async_copy(ys_hbm.at[pl.ds(row0, TOK_ROWS), :],
                                      ybuf.at[buf, k, pl.ds(r * TOK_PITCH, TOK_ROWS), :], sem.at[buf]).start()

    def wait_gather(buf):
        rows = pl.ds(0, tm * TOK_ROWS)
        for k in range(2):
            pltpu.make_async_copy(ys_hbm.at[rows, :], ybuf.at[buf, k, rows, :], sem.at[buf]).wait()

    def finish():
        rec = rec_ref[...]
        y1, y2 = (_load_token_major(lambda rows, k=k: ybuf[cur, k, rows, :], tm, TOK_PITCH) for k in range(2))
        h2 = h1_ref[...] + rec[:, R_W1:R_W1 + 1] * y1 + rec[:, R_W2:R_W2 + 1] * y2
        hn = _rms(h2, g_ref[...]).astype(_BF16)
        gate = jax.nn.sigmoid(_dot(hn, wpg_ref[...]) + bpg_ref[...])
        pe = _dot(p_ref[...].astype(_BF16), wple_ref[...])
        o_ref[...] = _rms(h2 + gate * pe, fg_ref[...])

    @pl.when(i == 0)
    def _():
        start_gather(0, 0)

    wait_gather(cur)

    @pl.when(i + 1 < pl.num_programs(0))
    def _():
        finish()
        start_gather(i + 1, 1 - cur)

    @pl.when(i + 1 == pl.num_programs(0))
    def _():
        finish()


def _ple(h1, ys, slots, rec, p, g, wpg, bpg, wple, fg, tm, row0):
    n = p.shape[0]
    off = row0 // tm
    tok = lambda width: pl.BlockSpec((tm, width), lambda i, s: (i + off, 0))
    const = lambda shape: pl.BlockSpec(shape, lambda i, s: (0,) * len(shape), pipeline_mode=pl.Buffered(1))
    grid_spec = pltpu.PrefetchScalarGridSpec(
        num_scalar_prefetch=1,
        grid=(n // tm,),
        in_specs=[tok(D_MODEL), tok(LANES), pl.BlockSpec((tm, D_PLE), lambda i, s: (i, 0)),
                  const((1, D_MODEL)), const((D_MODEL, D_MODEL)), const((1, D_MODEL)),
                  const((D_PLE, D_MODEL)), const((1, D_MODEL)), pl.BlockSpec(memory_space=pl.ANY)],
        out_specs=pl.BlockSpec((tm, D_MODEL), lambda i, s: (i, 0)),
        scratch_shapes=[pltpu.VMEM((2, 2, tm * TOK_PITCH, LANES), _F32), pltpu.SemaphoreType.DMA((2,))],
    )
    return pl.pallas_call(
        functools.partial(_ple_body, tm=tm),
        out_shape=jax.ShapeDtypeStruct((n, D_MODEL), _F32),
        grid_spec=grid_spec,
        compiler_params=_params("arbitrary"),
        name="ple",
    )(slots, h1, rec, p, g, wpg, bpg, wple, fg, ys)


def _schedule(rec, counts_row, n_all, n_tiles):
    ids = jnp.arange(N_EXPERTS, dtype=jnp.int32)
    expert = rec[:n_all, R_E1:R_E2 + 1].astype(jnp.int32)
    rank = rec[:n_all, R_RANK1:R_RANK2 + 1].astype(jnp.int32)
    counts = counts_row[0, :N_EXPERTS].astype(jnp.int32)
    tiles_per = (counts + TM_E - 1) // TM_E
    tiles_end = jnp.cumsum(tiles_per)
    n_used = tiles_end[-1]
    slot = jnp.sum(jnp.where(expert[..., None] == ids, (tiles_end - tiles_per) * TM_E, 0), axis=-1) + rank
    token = jnp.broadcast_to(jnp.arange(n_all, dtype=jnp.int32)[:, None], (n_all, 2))
    table = jnp.zeros((n_tiles * TM_E,), jnp.int32).at[slot.reshape(-1)].set(token.reshape(-1), unique_indices=True)
    tile_ids = jnp.arange(n_tiles, dtype=jnp.int32)
    te = jnp.minimum(jnp.sum((tiles_end[None, :] <= tile_ids[:, None]).astype(jnp.int32), axis=1), N_EXPERTS - 1)
    last_expert = jnp.sum(jnp.where(tile_ids == n_used - 1, te, 0))
    te = jnp.where(tile_ids < n_used, te, last_expert).astype(jnp.int32)
    return te, n_used.astype(jnp.int32).reshape(1), table, slot.reshape(-1)


def kernel(x_prompt, x_sample, state_conf_conv, state_lru_conv, state_lru_h, p_prompt, p_sample, norm1_g, w_in, b_in, conf_dw_w, conf_dw_b, conf_ln_g, conf_ln_b, lru_conv_w, lru_conv_b, lru_wa, lru_ba, lru_wx, lru_bx, lru_lambda, w_out, norm2_g, w_grp, b_grp, w_rt, b_rt, w_gate, w_up, w_down, ple_norm_g, w_ple, w_pg, b_pg, final_g):
    batch, seq, _ = x_prompt.shape
    nb = x_sample.shape[0]
    n_p = batch * seq
    n_all = n_p + nb
    layer = 0

    row = lambda v: v.reshape(1, -1)
    w_in_b = w_in[layer].astype(_BF16)
    wc_b = w_out[layer, :D_CONF].astype(_BF16)
    wr_b = w_out[layer, D_CONF:].astype(_BF16)
    w_router = jnp.concatenate(
        [w_grp[layer], w_rt[layer], jnp.zeros((D_MODEL, LANES - N_GROUPS - N_EXPERTS), _F32)], axis=1).astype(_BF16)
    b_router = jnp.concatenate([b_grp[layer], b_rt[layer], jnp.zeros((LANES - N_GROUPS - N_EXPERTS,), _F32)])
    w_gates = jnp.concatenate([lru_wa[layer], lru_wx[layer]], axis=-1).astype(_BF16)
    b_gates = jnp.concatenate([lru_ba[layer], lru_bx[layer]], axis=-1)
    w_pg_b = w_pg[layer].astype(_BF16)
    w_ple_b = w_ple[layer].astype(_BF16)
    conf_args = (conf_dw_w[layer], row(conf_dw_b[layer]), row(conf_ln_g[layer]), row(conf_ln_b[layer]))
    lru_args = (lru_conv_w[layer], row(lru_conv_b[layer]), w_gates, b_gates, row(lru_lambda[layer]))
    out_args = (wc_b, wr_b, row(norm2_g[layer]), w_router, row(b_router))
    ple_args = (row(ple_norm_g[layer]), w_pg_b, row(b_pg[layer]), w_ple_b, row(final_g))

    xp = x_prompt.reshape(n_p, D_MODEL)
    u_p, lx_p, lg_p = _inproj(xp, row(norm1_g[layer]), w_in_b, row(b_in[layer]), TM)
    c_p = _conf_prompt(u_p, *conf_args, batch, seq)
    r_p, hl_p = _lru_prompt(lx_p, lg_p, *lru_args, batch, seq)

    xs = x_sample.reshape(nb, D_MODEL)
    conf_buf = state_conf_conv[layer]
    lru_buf = state_lru_conv[layer]
    u_s, lx_s, lg_s = _inproj(xs, row(norm1_g[layer]), w_in_b, row(b_in[layer]), nb)
    c_s = _conf_step(conf_buf, u_s, *conf_args)
    r_s, hn_s = _lru_step(lru_buf[:, 0], lru_buf[:, 1], lru_buf[:, 2], lx_s, state_lru_h[layer], lg_s, *lru_args)
    h1, xn, rec, counts = _outproj(c_p, r_p, xp, c_s, r_s, xs, *out_args, TM)

    n_tiles = (2 * n_all + TM_E - 1) // TM_E + N_EXPERTS
    tile_expert, n_used, table, slots = _schedule(rec, counts, n_all, n_tiles)
    ys = _moe(tile_expert, n_used, table, xn, w_gate[layer], w_up[layer], w_down[layer])

    y_p = _ple(h1, ys, slots[:2 * n_p], rec, p_prompt[layer].reshape(n_p, D_PLE), *ple_args, TM_PLE, 0)
    y_s = _ple(h1, ys, slots[2 * n_p:], rec, p_sample[layer].reshape(nb, D_PLE), *ple_args, nb, n_p)

    u_p3 = u_p.reshape(batch, seq, D_CONF)
    lx_p3 = lx_p.reshape(batch, seq, D_LRU)
    return (
        y_p.reshape(batch, seq, D_MODEL),
        y_s.reshape(nb, 1, D_MODEL),
        u_p3[None, :, seq - (CONF_K - 1):],
        jnp.concatenate([conf_buf[:, 1:], u_s[:, None]], axis=1)[None],
        lx_p3[None, :, seq - (LRU_K - 1):],
        jnp.concatenate([lru_buf[:, 1:], lx_s[:, None]], axis=1)[None],
        hl_p.reshape(1, batch, D_LRU),
        hn_s[None],
    )
```

```python
import functools

import jax
import jax.numpy as jnp
from jax import lax
from jax.experimental import pallas as pl
from jax.experimental.pallas import tpu as pltpu

D_MODEL = 2048
D_CONF = 1024
D_LRU = 1024
N_HEADS = 8
HEAD = 128
CONF_K = 31
LRU_K = 4
LRU_C = 8.0
N_GROUPS = 4
EPG = 8
N_EXPERTS = 32
D_EXPERT = 512
D_PLE = 256
EPS = 1e-6

LANES = 128
SUBLANES = 8
VMEM_LIMIT = 56 * 1024 * 1024

TM = 512
TM_E = 256
CONF_TC = 64
LRU_TC = 256
LRU_PITCH = LRU_TC + SUBLANES
TM_PLE = 256
TOK_ROWS = D_MODEL // LANES
TOK_PITCH = TOK_ROWS + SUBLANES

_BF16 = jnp.bfloat16
_F32 = jnp.float32


def _params(*sem):
    return pltpu.CompilerParams(dimension_semantics=sem, vmem_limit_bytes=VMEM_LIMIT)


def _const_spec(shape):
    nd = len(shape)
    return pl.BlockSpec(shape, lambda *_: (0,) * nd, pipeline_mode=pl.Buffered(1))


def _store_token_major(ref, val, pitch=TOK_ROWS):
    for tb in range(val.shape[0] // SUBLANES):
        for s in range(TOK_ROWS):
            ref[pl.ds(tb * SUBLANES * pitch + s, SUBLANES, stride=pitch), :] = (
                val[tb * SUBLANES:(tb + 1) * SUBLANES, s * LANES:(s + 1) * LANES])


def _load_token_major(load, m, pitch):
    slabs = [jnp.concatenate([load(pl.ds(tb * SUBLANES * pitch + s, SUBLANES, stride=pitch))
                              for tb in range(m // SUBLANES)], axis=0) for s in range(TOK_ROWS)]
    return jnp.concatenate(slabs, axis=1)


def _rms(x, g):
    return x * lax.rsqrt(jnp.mean(x * x, axis=-1, keepdims=True) + EPS) * g


def _dot(a, b):
    return jnp.dot(a, b, preferred_element_type=_F32)


def _inproj_body(x_ref, g_ref, w_ref, b_ref, u_ref, lx_ref, lg_ref):
    xn = _rms(x_ref[...], g_ref[...]).astype(_BF16)

    def proj(k):
        sl = slice(k * D_CONF, (k + 1) * D_CONF)
        return _dot(xn, w_ref[:, sl]) + b_ref[:, sl]

    u_ref[...] = proj(0) * jax.nn.sigmoid(proj(1))
    lx_ref[...] = proj(2)
    lg_ref[...] = jax.nn.gelu(proj(3))


def _inproj(x, g, w, b, tm):
    n = x.shape[0]
    out = jax.ShapeDtypeStruct((n, D_CONF), _F32)
    tok = lambda width: pl.BlockSpec((tm, width), lambda i: (i, 0))
    return pl.pallas_call(
        _inproj_body,
        out_shape=(out, out, out),
        grid=(n // tm,),
        in_specs=[tok(D_MODEL), _const_spec((1, D_MODEL)), _const_spec((D_MODEL, 4 * D_CONF)),
                  _const_spec((1, 4 * D_CONF))],
        out_specs=(tok(D_CONF), tok(D_CONF), tok(D_CONF)),
        compiler_params=_params("arbitrary"),
        name="inproj",
    )(x, g, w, b)


def _layernorm_silu(conv, g, b):
    mu = jnp.mean(conv, axis=-1, keepdims=True)
    cen = conv - mu
    var = jnp.mean(cen * cen, axis=-1, keepdims=True)
    y = cen * lax.rsqrt(var + EPS) * g + b
    return y * jax.nn.sigmoid(y)


def _conf_body(u_ref, w_ref, b_ref, g_ref, lb_ref, c_ref, s_ref, *, seq):
    head = 4 * SUBLANES
    s_ref[0:head, :] = jnp.zeros((head, D_CONF), _F32)
    s_ref[head:head + seq, :] = u_ref[...]
    s_ref[head + seq:, :] = jnp.zeros((SUBLANES, D_CONF), _F32)
    tc = CONF_TC

    def chunk(ci, carry):
        t0 = pl.multiple_of(ci * tc, tc)
        win = s_ref[pl.ds(t0, tc + 5 * SUBLANES), :]
        out = None
        for r in range(SUBLANES):
            part = None
            for q in range(5):
                m = SUBLANES * q + r
                if 2 <= m <= CONF_K + 1:
                    term = win[SUBLANES * q:SUBLANES * q + tc + SUBLANES, :] * w_ref[m - 2:m - 1, :]
                    part = term if part is None else part + term
            shifted = part[r:r + tc, :]
            out = shifted if out is None else out + shifted
        c = _layernorm_silu(out + b_ref[...], g_ref[...], lb_ref[...])
        c_ref[pl.ds(t0, tc), :] = c.astype(_BF16)
        return carry

    lax.fori_loop(0, seq // tc, chunk, 0)


def _conf_prompt(u, w, b, g, lb, batch, seq):
    return pl.pallas_call(
        functools.partial(_conf_body, seq=seq),
        out_shape=jax.ShapeDtypeStruct((batch * seq, D_CONF), _BF16),
        grid=(batch,),
        in_specs=[pl.BlockSpec((seq, D_CONF), lambda i: (i, 0)), _const_spec((CONF_K, D_CONF)),
                  _const_spec((1, D_CONF)), _const_spec((1, D_CONF)), _const_spec((1, D_CONF))],
        out_specs=pl.BlockSpec((seq, D_CONF), lambda i: (i, 0)),
        scratch_shapes=[pltpu.VMEM((seq + 5 * SUBLANES, D_CONF), _F32)],
        compiler_params=_params("arbitrary"),
        name="conf_prompt",
    )(u, w, b, g, lb)


def _conf_step_body(buf_ref, u_ref, w_ref, b_ref, g_ref, lb_ref, c_ref):
    w = w_ref[...]
    conv = jnp.sum(buf_ref[...] * w[None, :CONF_K - 1, :], axis=1) + u_ref[...] * w[CONF_K - 1:CONF_K, :]
    c_ref[...] = _layernorm_silu(conv + b_ref[...], g_ref[...], lb_ref[...]).astype(_BF16)


def _conf_step(buf, u, w, b, g, lb):
    nb = buf.shape[0]
    bb = 16
    return pl.pallas_call(
        _conf_step_body,
        out_shape=jax.ShapeDtypeStruct((nb, D_CONF), _BF16),
        grid=(nb // bb,),
        in_specs=[pl.BlockSpec((bb, CONF_K - 1, D_CONF), lambda i: (i, 0, 0)),
                  pl.BlockSpec((bb, D_CONF), lambda i: (i, 0)), _const_spec((CONF_K, D_CONF)),
                  _const_spec((1, D_CONF)), _const_spec((1, D_CONF)), _const_spec((1, D_CONF))],
        out_specs=pl.BlockSpec((bb, D_CONF), lambda i: (i, 0)),
        compiler_params=_params("arbitrary"),
        name="conf_step",
    )(buf, u, w, b, g, lb)


def _lru_gates(lx, wg_ref, bg_ref, lam_ref, head):
    sl = slice(head * HEAD, (head + 1) * HEAD)
    lxh = lx[:, sl]
    z = _dot(lxh.astype(_BF16), wg_ref[head]) + bg_ref[head:head + 1, :]
    r = jax.nn.sigmoid(z[:, :HEAD])
    i = jax.nn.sigmoid(z[:, HEAD:])
    log_a = -LRU_C * r * jax.nn.softplus(-lam_ref[:, sl])
    a = jnp.exp(log_a)
    return a, jnp.sqrt(-jnp.tanh(log_a) * (a * a + 1.0)) * i * lxh


def _lru_body(lx_ref, lg_ref, cw_ref, cb_ref, wg_ref, bg_ref, lam_ref, r_ref, hl_ref,
              halo_ref, h_ref, a_ref, u_ref, hs_ref):
    j = pl.program_id(1)
    tc = LRU_TC

    @pl.when(j == 0)
    def _():
        halo_ref[...] = jnp.zeros_like(halo_ref)
        h_ref[...] = jnp.zeros_like(h_ref)

    cur = lx_ref[...]
    ext = jnp.concatenate([halo_ref[...], cur], axis=0)
    lx = cb_ref[...] + cw_ref[LRU_K - 1:LRU_K, :] * cur
    for back in range(1, LRU_K):
        lx = lx + cw_ref[LRU_K - 1 - back:LRU_K - back, :] * ext[SUBLANES - back:SUBLANES - back + tc, :]
    halo_ref[...] = cur[tc - SUBLANES:, :]

    for head in range(N_HEADS):
        a, u = _lru_gates(lx, wg_ref, bg_ref, lam_ref, head)
        a_ref[head * LRU_PITCH:head * LRU_PITCH + tc, :] = a
        u_ref[head * LRU_PITCH:head * LRU_PITCH + tc, :] = u

    def step(t, h):
        rows = pl.ds(t, N_HEADS, stride=LRU_PITCH)
        h = a_ref[rows, :] * h + u_ref[rows, :]
        hs_ref[rows, :] = h
        return h

    h_ref[...] = lax.fori_loop(0, tc, step, h_ref[...], unroll=8)

    for head in range(N_HEADS):
        sl = slice(head * HEAD, (head + 1) * HEAD)
        hs = hs_ref[head * LRU_PITCH:head * LRU_PITCH + tc, :]
        r_ref[:, sl] = (hs * lg_ref[:, sl]).astype(_BF16)

    @pl.when(j == pl.num_programs(1) - 1)
    def _():
        hl_ref[0] = h_ref[...]


def _lru_prompt(lx, lg, cw, cb, wg, bg, lam, batch, seq):
    nchunk = seq // LRU_TC
    tok = pl.BlockSpec((LRU_TC, D_LRU), lambda b, j: (b * nchunk + j, 0))
    pitch_rows = N_HEADS * LRU_PITCH
    return pl.pallas_call(
        _lru_body,
        out_shape=(jax.ShapeDtypeStruct((batch * seq, D_LRU), _BF16),
                   jax.ShapeDtypeStruct((batch, N_HEADS, HEAD), _F32)),
        grid=(batch, nchunk),
        in_specs=[tok, tok, _const_spec((LRU_K, D_LRU)), _const_spec((1, D_LRU)),
                  _const_spec((N_HEADS, HEAD, 2 * HEAD)), _const_spec((N_HEADS, 2 * HEAD)),
                  _const_spec((1, D_LRU))],
        out_specs=(tok, pl.BlockSpec((1, N_HEADS, HEAD), lambda b, j: (b, 0, 0))),
        scratch_shapes=[pltpu.VMEM((SUBLANES, D_LRU), _F32), pltpu.VMEM((N_HEADS, HEAD), _F32),
                        pltpu.VMEM((pitch_rows, HEAD), _F32), pltpu.VMEM((pitch_rows, HEAD), _F32),
                        pltpu.VMEM((pitch_rows, HEAD), _F32)],
        compiler_params=_params("arbitrary", "arbitrary"),
        name="lru_prompt",
    )(lx, lg, cw, cb, wg, bg, lam)


def _lru_step_body(b0_ref, b1_ref, b2_ref, lx_ref, h0_ref, lg_ref, cw_ref, cb_ref, wg_ref, bg_ref, lam_ref,
                   r_ref, hn_ref):
    lx = (cb_ref[...] + cw_ref[0:1, :] * b0_ref[...] + cw_ref[1:2, :] * b1_ref[...]
          + cw_ref[2:3, :] * b2_ref[...] + cw_ref[3:4, :] * lx_ref[...])
    for head in range(N_HEADS):
        sl = slice(head * HEAD, (head + 1) * HEAD)
        a, u = _lru_gates(lx, wg_ref, bg_ref, lam_ref, head)
        h = a * h0_ref[:, sl] + u
        hn_ref[:, sl] = h
        r_ref[:, sl] = (h * lg_ref[:, sl]).astype(_BF16)


def _lru_step(b0, b1, b2, lx, h0, lg, cw, cb, wg, bg, lam):
    nb = lx.shape[0]
    full = _const_spec((nb, D_LRU))
    return pl.pallas_call(
        _lru_step_body,
        out_shape=(jax.ShapeDtypeStruct((nb, D_LRU), _BF16), jax.ShapeDtypeStruct((nb, D_LRU), _F32)),
        grid=(1,),
        in_specs=[full] * 6 + [_const_spec((LRU_K, D_LRU)), _const_spec((1, D_LRU)),
                               _const_spec((N_HEADS, HEAD, 2 * HEAD)), _const_spec((N_HEADS, 2 * HEAD)),
                               _const_spec((1, D_LRU))],
        out_specs=(pl.BlockSpec((nb, D_LRU), lambda i: (0, 0)), pl.BlockSpec((nb, D_LRU), lambda i: (0, 0))),
        compiler_params=_params("arbitrary"),
        name="lru_step",
    )(b0, b1, b2, lx, h0, lg, cw, cb, wg, bg, lam)


R_E1, R_E2, R_W1, R_W2, R_RANK1, R_RANK2 = range(6)


def _route_tile(logits, seen):
    m = logits.shape[0]
    lane = lax.broadcasted_iota(jnp.int32, (m, LANES), 1).astype(_F32)
    neg = jnp.float32(-jnp.inf)

    def first_max(vals):
        vmax = jnp.max(vals, axis=1, keepdims=True)
        return vmax, jnp.min(jnp.where(vals == vmax, lane, float(LANES)), axis=1, keepdims=True)

    in_groups = lane < N_GROUPS
    g_max, g_idx = first_max(jnp.where(in_groups, logits, neg))
    g_den = jnp.sum(jnp.where(in_groups, jnp.exp(logits - g_max), 0.0), axis=1, keepdims=True)
    lo = N_GROUPS + EPG * g_idx
    e_logits = jnp.where((lane >= lo) & (lane < lo + EPG), logits, neg)
    v1, i1 = first_max(e_logits)
    v2, i2 = first_max(jnp.where(lane == i1, neg, e_logits))
    t = jnp.exp(v2 - v1)
    w1 = 1.0 / (g_den * (1.0 + t))
    w2 = w1 * t
    e1 = i1 - N_GROUPS
    e2 = i2 - N_GROUPS

    is1 = lane == e1
    is2 = lane == e2
    onehot = jnp.where(is1 | is2, 1.0, 0.0)
    earlier = (lax.broadcasted_iota(jnp.int32, (m, m), 1) < lax.broadcasted_iota(jnp.int32, (m, m), 0))
    before = seen + _dot(jnp.where(earlier, 1.0, 0.0).astype(_BF16), onehot.astype(_BF16))
    rank1 = jnp.sum(jnp.where(is1, before, 0.0), axis=1, keepdims=True)
    rank2 = jnp.sum(jnp.where(is2, before, 0.0), axis=1, keepdims=True)

    rec = jnp.zeros((m, LANES), _F32)
    for pos, val in ((R_E1, e1), (R_E2, e2), (R_W1, w1), (R_W2, w2), (R_RANK1, rank1), (R_RANK2, rank2)):
        rec = jnp.where(lane == pos, val, rec)
    return rec, seen + jnp.sum(onehot, axis=0, keepdims=True)


def _outproj_body(cp_ref, rp_ref, xp_ref, cs_ref, rs_ref, xs_ref, wc_ref, wr_ref, g_ref, wrt_ref, brt_ref,
                  h1_ref, xn_ref, rec_ref, cnt_ref, seen_ref, *, n_ptiles, nb):
    i = pl.program_id(0)

    @pl.when(i == 0)
    def _():
        seen_ref[...] = jnp.zeros_like(seen_ref)

    def mix(c, r, x):
        h1 = x + _dot(c, wc_ref[...]) + _dot(r, wr_ref[...])
        xn = _rms(h1, g_ref[...])
        logits = _dot(xn.astype(_BF16), wrt_ref[...]) + brt_ref[...]
        rec, seen = _route_tile(logits, seen_ref[...])
        seen_ref[...] = seen
        return h1, xn, rec

    @pl.when(i < n_ptiles)
    def _():
        h1_ref[...], xn_ref[...], rec_ref[...] = mix(cp_ref[...], rp_ref[...], xp_ref[...])

    @pl.when(i == n_ptiles)
    def _():
        for ref, val in zip((h1_ref, xn_ref, rec_ref), mix(cs_ref[...], rs_ref[...], xs_ref[...])):
            ref[:nb, :] = val
            ref[nb:, :] = jnp.zeros((ref.shape[0] - nb, ref.shape[1]), _F32)

    cnt_ref[...] = jnp.broadcast_to(seen_ref[...], cnt_ref.shape)


def _outproj(cp, rp, xp, cs, rs, xs, wc, wr, g, wrt, brt, tm):
    n_ptiles = xp.shape[0] // tm
    nb = xs.shape[0]
    rows = (n_ptiles + 1) * tm
    last = n_ptiles - 1
    tok = lambda width: pl.BlockSpec((tm, width), lambda i: (jnp.minimum(i, last), 0))
    out_tok = lambda width: pl.BlockSpec((tm, width), lambda i: (i, 0))
    return pl.pallas_call(
        functools.partial(_outproj_body, n_ptiles=n_ptiles, nb=nb),
        out_shape=(jax.ShapeDtypeStruct((rows, D_MODEL), _F32), jax.ShapeDtypeStruct((rows, D_MODEL), _F32),
                   jax.ShapeDtypeStruct((rows, LANES), _F32), jax.ShapeDtypeStruct((SUBLANES, LANES), _F32)),
        grid=(n_ptiles + 1,),
        in_specs=[tok(D_CONF), tok(D_LRU), tok(D_MODEL), _const_spec((nb, D_CONF)), _const_spec((nb, D_LRU)),
                  _const_spec((nb, D_MODEL)), _const_spec((D_CONF, D_MODEL)), _const_spec((D_LRU, D_MODEL)),
                  _const_spec((1, D_MODEL)), _const_spec((D_MODEL, LANES)), _const_spec((1, LANES))],
        out_specs=(out_tok(D_MODEL), out_tok(D_MODEL), out_tok(LANES),
                   pl.BlockSpec((SUBLANES, LANES), lambda i: (0, 0))),
        scratch_shapes=[pltpu.VMEM((1, LANES), _F32)],
        compiler_params=_params("arbitrary"),
        name="outproj",
    )(cp, rp, xp, cs, rs, xs, wc, wr, g, wrt, brt)


def _dispatch_body(slot_ref, lo_ref, hi_ref, x_ref, xs_hbm, zeros, sem, zsem, *, tm, n_full, tail):
    i = pl.program_id(0)

    def zero_copies(n, wait):
        lo, hi = lo_ref[n], hi_ref[n]
        head_end = jnp.minimum(((lo + SUBLANES - 1) // SUBLANES) * SUBLANES, hi)

        def go(copy):
            copy.wait() if wait else copy.start()

        def single(s, carry):
            go(pltpu.make_async_copy(zeros.at[pl.ds(0, 1), :], xs_hbm.at[pl.ds(s, 1), :], zsem))
            return carry
        lax.fori_loop(lo, head_end, single, 0)

        size = hi - head_end
        n_bufs = size // TM_E

        def whole(c, carry):
            start = pl.multiple_of(head_end + c * TM_E, SUBLANES)
            go(pltpu.make_async_copy(zeros, xs_hbm.at[pl.ds(start, TM_E), :], zsem))
            return carry
        lax.fori_loop(0, n_bufs, whole, 0)
        start = pl.multiple_of(head_end + n_bufs * TM_E, SUBLANES)
        rest = pl.multiple_of(size - n_bufs * TM_E, SUBLANES)

        @pl.when(rest > 0)
        def _():
            go(pltpu.make_async_copy(zeros.at[pl.ds(0, rest), :], xs_hbm.at[pl.ds(start, rest), :], zsem))

    def scatter(rows):
        for r in range(rows):
            for k in range(2):
                s = slot_ref[(i * tm + r) * 2 + k]
                pltpu.make_async_copy(x_ref.at[pl.ds(r, 1), :], xs_hbm.at[pl.ds(s, 1), :], sem).start()
        for k in range(2):
            pltpu.make_async_copy(x_ref.at[pl.ds(0, rows), :], xs_hbm.at[pl.ds(0, rows), :], sem).wait()

    @pl.when(i == 0)
    def _():
        zeros[...] = jnp.zeros_like(zeros)
        for wait in (False, True):
            lax.fori_loop(0, N_EXPERTS + 1, lambda n, c: (zero_copies(n, wait), c)[1], 0)

    @pl.when(i < n_full)
    def _():
        scatter(tm)

    @pl.when(i == n_full)
    def _():
        scatter(tail)


def _dispatch(slots, pad_lo, pad_hi, xn, n_all, n_rows, tm):
    n_full, tail = divmod(n_all, tm)
    grid_spec = pltpu.PrefetchScalarGridSpec(
        num_scalar_prefetch=3,
        grid=(n_full + (1 if tail else 0),),
        in_specs=[pl.BlockSpec((tm, D_MODEL), lambda i, s, lo, hi: (i, 0))],
        out_specs=pl.BlockSpec(memory_space=pl.ANY),
        scratch_shapes=[pltpu.VMEM((TM_E, D_MODEL), _F32), pltpu.SemaphoreType.DMA(()),
                        pltpu.SemaphoreType.DMA(())],
    )
    return pl.pallas_call(
        functools.partial(_dispatch_body, tm=tm, n_full=n_full, tail=tail),
        out_shape=jax.ShapeDtypeStruct((n_rows, D_MODEL), _F32),
        grid_spec=grid_spec,
        compiler_params=_params("arbitrary"),
        name="dispatch",
    )(slots, pad_lo, pad_hi, xn)


def _moe_body(te_ref, nx_ref, nu_ref, x_ref, wg_hbm, wu_hbm, wd_hbm, y_ref,
              wg_f, wu_f, wd_f, wg_b, wu_b, wd_b, wsem):
    j = pl.program_id(0)
    n_used = nu_ref[0]

    def weight_copies(e):
        return (pltpu.make_async_copy(wg_hbm.at[e], wg_f, wsem.at[0]),
                pltpu.make_async_copy(wu_hbm.at[e], wu_f, wsem.at[1]),
                pltpu.make_async_copy(wd_hbm.at[e], wd_f, wsem.at[2]))

    @pl.when(j == 0)
    def _():
        for copy in weight_copies(te_ref[0]):
            copy.start()

    @pl.when(j < n_used)
    def _():
        @pl.when((j == 0) | (te_ref[j] != te_ref[jnp.maximum(j - 1, 0)]))
        def _():
            for copy in weight_copies(te_ref[j]):
                copy.wait()
            wg_b[...] = wg_f[...].astype(_BF16)
            wu_b[...] = wu_f[...].astype(_BF16)
            wd_b[...] = wd_f[...].astype(_BF16)

            @pl.when(nx_ref[j] >= 0)
            def _():
                for copy in weight_copies(nx_ref[j]):
                    copy.start()

        x = x_ref[...].astype(_BF16)
        g = _dot(x, wg_b[...])
        u = _dot(x, wu_b[...])
        hid = g * jax.nn.sigmoid(g) * u
        _store_token_major(y_ref, _dot(hid.astype(_BF16), wd_b[...]))

    @pl.when(j >= n_used)
    def _():
        y_ref[...] = jnp.zeros_like(y_ref)


def _moe(tile_expert, next_expert, n_used, xs, w_gate, w_up, w_down):
    n_tiles = tile_expert.shape[0]
    any_space = pl.BlockSpec(memory_space=pl.ANY)
    up_shape, down_shape = (D_MODEL, D_EXPERT), (D_EXPERT, D_MODEL)
    grid_spec = pltpu.PrefetchScalarGridSpec(
        num_scalar_prefetch=3,
        grid=(n_tiles,),
        in_specs=[pl.BlockSpec((TM_E, D_MODEL), lambda j, te, nx, nu: (j, 0)), any_space, any_space, any_space],
        out_specs=pl.BlockSpec((TM_E * TOK_ROWS, LANES), lambda j, te, nx, nu: (j, 0)),
        scratch_shapes=[pltpu.VMEM(up_shape, _F32), pltpu.VMEM(up_shape, _F32), pltpu.VMEM(down_shape, _F32),
                        pltpu.VMEM(up_shape, _BF16), pltpu.VMEM(up_shape, _BF16), pltpu.VMEM(down_shape, _BF16),
                        pltpu.SemaphoreType.DMA((3,))],
    )
    return pl.pallas_call(
        _moe_body,
        out_shape=jax.ShapeDtypeStruct((n_tiles * TM_E * TOK_ROWS, LANES), _F32),
        grid_spec=grid_spec,
        compiler_params=_params("arbitrary"),
        name="moe",
    )(tile_expert, next_expert, n_used, xs, w_gate, w_up, w_down)


def _ple_body(slot_ref, h1_ref, rec_ref, p_ref, g_ref, wpg_ref, bpg_ref, wple_ref, fg_ref, ys_hbm, o_ref,
              ybuf, sem, *, tm):
    i = pl.program_id(0)
    cur = lax.rem(i, 2)

    def start_gather(tile, buf):
        for r in range(tm):
            for k in range(2):
                row0 = pl.multiple_of(slot_ref[(tile * tm + r) * 2 + k] * TOK_ROWS, TOK_ROWS)
                pltpu.make_async_copy(ys_hbm.at[pl.ds(row0, TOK_ROWS), :],
                                      ybuf.at[buf, k, pl.ds(r * TOK_PITCH, TOK_ROWS), :], sem.at[buf]).start()

    def wait_gather(buf):
        rows = pl.ds(0, tm * TOK_ROWS)
        for k in range(2):
            pltpu.make_async_copy(ys_hbm.at[rows, :], ybuf.at[buf, k, rows, :], sem.at[buf]).wait()

    def finish():
        rec = rec_ref[...]
        y1, y2 = (_load_token_major(lambda rows, k=k: ybuf[cur, k, rows, :], tm, TOK_PITCH) for k in range(2))
        h2 = h1_ref[...] + rec[:, R_W1:R_W1 + 1] * y1 + rec[:, R_W2:R_W2 + 1] * y2
        hn = _rms(h2, g_ref[...]).astype(_BF16)
        gate = jax.nn.sigmoid(_dot(hn, wpg_ref[...]) + bpg_ref[...])
        pe = _dot(p_ref[...].astype(_BF16), wple_ref[...])
        o_ref[...] = _rms(h2 + gate * pe, fg_ref[...])

    @pl.when(i == 0)
    def _():
        start_gather(0, 0)

    wait_gather(cur)

    @pl.when(i + 1 < pl.num_programs(0))
    def _():
        finish()
        start_gather(i + 1, 1 - cur)

    @pl.when(i + 1 == pl.num_programs(0))
    def _():
        finish()


def _ple(h1, ys, slots, rec, p, g, wpg, bpg, wple, fg, tm, row0):
    n = p.shape[0]
    off = row0 // tm
    tok = lambda width: pl.BlockSpec((tm, width), lambda i, s: (i + off, 0))
    const = lambda shape: pl.BlockSpec(shape, lambda i, s: (0,) * len(shape), pipeline_mode=pl.Buffered(1))
    grid_spec = pltpu.PrefetchScalarGridSpec(
        num_scalar_prefetch=1,
        grid=(n // tm,),
        in_specs=[tok(D_MODEL), tok(LANES), pl.BlockSpec((tm, D_PLE), lambda i, s: (i, 0)),
                  const((1, D_MODEL)), const((D_MODEL, D_MODEL)), const((1, D_MODEL)),
                  const((D_PLE, D_MODEL)), const((1, D_MODEL)), pl.BlockSpec(memory_space=pl.ANY)],
        out_specs=pl.BlockSpec((tm, D_MODEL), lambda i, s: (i, 0)),
        scratch_shapes=[pltpu.VMEM((2, 2, tm * TOK_PITCH, LANES), _F32), pltpu.SemaphoreType.DMA((2,))],
    )
    return pl.pallas_call(
        functools.partial(_ple_body, tm=tm),
        out_shape=jax.ShapeDtypeStruct((n, D_MODEL), _F32),
        grid_spec=grid_spec,
        compiler_params=_params("arbitrary"),
        name="ple",
    )(slots, h1, rec, p, g, wpg, bpg, wple, fg, ys)


def _schedule(rec, counts_row, n_all, n_tiles):
    ids = jnp.arange(N_EXPERTS, dtype=jnp.int32)
    expert = rec[:n_all, R_E1:R_E2 + 1].astype(jnp.int32)
    rank = rec[:n_all, R_RANK1:R_RANK2 + 1].astype(jnp.int32)
    counts = counts_row[0, :N_EXPERTS].astype(jnp.int32)
    tiles_per = (counts + TM_E - 1) // TM_E
    tiles_end = jnp.cumsum(tiles_per)
    n_used = tiles_end[-1]
    first_row = (tiles_end - tiles_per) * TM_E
    slot = jnp.sum(jnp.where(expert[..., None] == ids, first_row, 0), axis=-1) + rank
    pad_lo = jnp.concatenate([first_row + counts, (n_used * TM_E).reshape(1)]).astype(jnp.int32)
    pad_hi = jnp.concatenate([tiles_end * TM_E, jnp.full((1,), n_tiles * TM_E)]).astype(jnp.int32)
    tile_ids = jnp.arange(n_tiles, dtype=jnp.int32)
    te = jnp.minimum(jnp.sum((tiles_end[None, :] <= tile_ids[:, None]).astype(jnp.int32), axis=1), N_EXPERTS - 1)
    last_expert = jnp.sum(jnp.where(tile_ids == n_used - 1, te, 0))
    te = jnp.where(tile_ids < n_used, te, last_expert).astype(jnp.int32)
    later = (ids[None, :] > te[:, None]) & (counts[None, :] > 0)
    nxt = jnp.min(jnp.where(later, ids[None, :], N_EXPERTS), axis=1)
    nxt = jnp.where(nxt < N_EXPERTS, nxt, -1).astype(jnp.int32)
    return te, nxt, n_used.astype(jnp.int32).reshape(1), slot.reshape(-1), pad_lo, pad_hi


def kernel(x_prompt, x_sample, state_conf_conv, state_lru_conv, state_lru_h, p_prompt, p_sample, norm1_g, w_in, b_in, conf_dw_w, conf_dw_b, conf_ln_g, conf_ln_b, lru_conv_w, lru_conv_b, lru_wa, lru_ba, lru_wx, lru_bx, lru_lambda, w_out, norm2_g, w_grp, b_grp, w_rt, b_rt, w_gate, w_up, w_down, ple_norm_g, w_ple, w_pg, b_pg, final_g):
    batch, seq, _ = x_prompt.shape
    nb = x_sample.shape[0]
    n_p = batch * seq
    n_all = n_p + nb
    layer = 0

    row = lambda v: v.reshape(1, -1)
    w_in_b = w_in[layer].astype(_BF16)
    wc_b = w_out[layer, :D_CONF].astype(_BF16)
    wr_b = w_out[layer, D_CONF:].astype(_BF16)
    w_router = jnp.concatenate(
        [w_grp[layer], w_rt[layer], jnp.zeros((D_MODEL, LANES - N_GROUPS - N_EXPERTS), _F32)], axis=1).astype(_BF16)
    b_router = jnp.concatenate([b_grp[layer], b_rt[layer], jnp.zeros((LANES - N_GROUPS - N_EXPERTS,), _F32)])
    w_gates = jnp.concatenate([lru_wa[layer], lru_wx[layer]], axis=-1).astype(_BF16)
    b_gates = jnp.concatenate([lru_ba[layer], lru_bx[layer]], axis=-1)
    w_pg_b = w_pg[layer].astype(_BF16)
    w_ple_b = w_ple[layer].astype(_BF16)
    conf_args = (conf_dw_w[layer], row(conf_dw_b[layer]), row(conf_ln_g[layer]), row(conf_ln_b[layer]))
    lru_args = (lru_conv_w[layer], row(lru_conv_b[layer]), w_gates, b_gates, row(lru_lambda[layer]))
    out_args = (wc_b, wr_b, row(norm2_g[layer]), w_router, row(b_router))
    ple_args = (row(ple_norm_g[layer]), w_pg_b, row(b_pg[layer]), w_ple_b, row(final_g))

    xp = x_prompt.reshape(n_p, D_MODEL)
    u_p, lx_p, lg_p = _inproj(xp, row(norm1_g[layer]), w_in_b, row(b_in[layer]), TM)
    c_p = _conf_prompt(u_p, *conf_args, batch, seq)
    r_p, hl_p = _lru_prompt(lx_p, lg_p, *lru_args, batch, seq)

    xs = x_sample.reshape(nb, D_MODEL)
    conf_buf = state_conf_conv[layer]
    lru_buf = state_lru_conv[layer]
    u_s, lx_s, lg_s = _inproj(xs, row(norm1_g[layer]), w_in_b, row(b_in[layer]), nb)
    c_s = _conf_step(conf_buf, u_s, *conf_args)
    r_s, hn_s = _lru_step(lru_buf[:, 0], lru_buf[:, 1], lru_buf[:, 2], lx_s, state_lru_h[layer], lg_s, *lru_args)
    h1, xn, rec, counts = _outproj(c_p, r_p, xp, c_s, r_s, xs, *out_args, TM)

    n_tiles = (2 * n_all + TM_E - 1) // TM_E + N_EXPERTS
    tile_expert, next_expert, n_used, slots, pad_lo, pad_hi = _schedule(rec, counts, n_all, n_tiles)
    xs = _dispatch(slots, pad_lo, pad_hi, xn, n_all, n_tiles * TM_E, TM)
    ys = _moe(tile_expert, next_expert, n_used, xs, w_gate[layer], w_up[layer], w_down[layer])

    y_p = _ple(h1, ys, slots[:2 * n_p], rec, p_prompt[layer].reshape(n_p, D_PLE), *ple_args, TM_PLE, 0)
    y_s = _ple(h1, ys, slots[2 * n_p:], rec, p_sample[layer].reshape(nb, D_PLE), *ple_args, nb, n_p)

    u_p3 = u_p.reshape(batch, seq, D_CONF)
    lx_p3 = lx_p.reshape(batch, seq, D_LRU)
    return (
        y_p.reshape(batch, seq, D_MODEL),
        y_s.reshape(nb, 1, D_MODEL),
        u_p3[None, :, seq - (CONF_K - 1):],
        jnp.concatenate([conf_buf[:, 1:], u_s[:, None]], axis=1)[None],
        lx_p3[None, :, seq - (LRU_K - 1):],
        jnp.concatenate([lru_buf[:, 1:], lx_s[:, None]], axis=1)[None],
        hl_p.reshape(1, batch, D_LRU),
        hn_s[None],
    )
```

```python
import functools

import jax
import jax.numpy as jnp
from jax import lax
from jax.experimental import pallas as pl
from jax.experimental.pallas import tpu as pltpu

D_MODEL = 2048
D_CONF = 1024
D_LRU = 1024
N_HEADS = 8
HEAD = 128
CONF_K = 31
LRU_K = 4
LRU_C = 8.0
N_GROUPS = 4
EPG = 8
N_EXPERTS = 32
D_EXPERT = 512
D_PLE = 256
EPS = 1e-6

LANES = 128
SUBLANES = 8
VMEM_LIMIT = 56 * 1024 * 1024

TM = 512
TM_E = 256
CONF_TC = 128
LRU_TC = 512
LRU_PITCH = LRU_TC + SUBLANES
TM_PLE = 256
TOK_ROWS = D_MODEL // LANES
TOK_PITCH = TOK_ROWS + SUBLANES

_BF16 = jnp.bfloat16
_F32 = jnp.float32


def _params(*sem):
    return pltpu.CompilerParams(dimension_semantics=sem, vmem_limit_bytes=VMEM_LIMIT)


def _const_spec(shape):
    nd = len(shape)
    return pl.BlockSpec(shape, lambda *_: (0,) * nd, pipeline_mode=pl.Buffered(1))


def _store_token_major(ref, val, pitch=TOK_ROWS):
    for tb in range(val.shape[0] // SUBLANES):
        for s in range(TOK_ROWS):
            ref[pl.ds(tb * SUBLANES * pitch + s, SUBLANES, stride=pitch), :] = (
                val[tb * SUBLANES:(tb + 1) * SUBLANES, s * LANES:(s + 1) * LANES])


def _load_token_major(load, m, pitch):
    slabs = [jnp.concatenate([load(pl.ds(tb * SUBLANES * pitch + s, SUBLANES, stride=pitch))
                              for tb in range(m // SUBLANES)], axis=0) for s in range(TOK_ROWS)]
    return jnp.concatenate(slabs, axis=1)


def _rms(x, g):
    return x * lax.rsqrt(jnp.mean(x * x, axis=-1, keepdims=True) + EPS) * g


def _dot(a, b):
    return jnp.dot(a, b, preferred_element_type=_F32)


def _inproj_body(x_ref, g_ref, w_ref, b_ref, u_ref, lx_ref, lg_ref):
    xn = _rms(x_ref[...], g_ref[...]).astype(_BF16)

    def proj(k):
        sl = slice(k * D_CONF, (k + 1) * D_CONF)
        return _dot(xn, w_ref[:, sl]) + b_ref[:, sl]

    u_ref[...] = proj(0) * jax.nn.sigmoid(proj(1))
    lx_ref[...] = proj(2)
    lg_ref[...] = jax.nn.gelu(proj(3))


def _inproj(x, g, w, b, tm):
    n = x.shape[0]
    out = jax.ShapeDtypeStruct((n, D_CONF), _F32)
    tok = lambda width: pl.BlockSpec((tm, width), lambda i: (i, 0))
    return pl.pallas_call(
        _inproj_body,
        out_shape=(out, out, out),
        grid=(n // tm,),
        in_specs=[tok(D_MODEL), _const_spec((1, D_MODEL)), _const_spec((D_MODEL, 4 * D_CONF)),
                  _const_spec((1, 4 * D_CONF))],
        out_specs=(tok(D_CONF), tok(D_CONF), tok(D_CONF)),
        compiler_params=_params("arbitrary"),
        name="inproj",
    )(x, g, w, b)


def _layernorm_silu(conv, g, b):
    mu = jnp.mean(conv, axis=-1, keepdims=True)
    cen = conv - mu
    var = jnp.mean(cen * cen, axis=-1, keepdims=True)
    y = cen * lax.rsqrt(var + EPS) * g + b
    return y * jax.nn.sigmoid(y)


def _conf_body(u_ref, w_ref, b_ref, g_ref, lb_ref, c_ref, s_ref, *, seq):
    head = 4 * SUBLANES
    s_ref[0:head, :] = jnp.zeros((head, D_CONF), _F32)
    s_ref[head:head + seq, :] = u_ref[...]
    s_ref[head + seq:, :] = jnp.zeros((SUBLANES, D_CONF), _F32)
    tc = CONF_TC

    def chunk(ci, carry):
        t0 = pl.multiple_of(ci * tc, tc)
        win = s_ref[pl.ds(t0, tc + 5 * SUBLANES), :]
        out = None
        for r in range(SUBLANES):
            part = None
            for q in range(5):
                m = SUBLANES * q + r
                if 2 <= m <= CONF_K + 1:
                    term = win[SUBLANES * q:SUBLANES * q + tc + SUBLANES, :] * w_ref[m - 2:m - 1, :]
                    part = term if part is None else part + term
            shifted = part[r:r + tc, :]
            out = shifted if out is None else out + shifted
        c = _layernorm_silu(out + b_ref[...], g_ref[...], lb_ref[...])
        c_ref[pl.ds(t0, tc), :] = c.astype(_BF16)
        return carry

    lax.fori_loop(0, seq // tc, chunk, 0)


def _conf_prompt(u, w, b, g, lb, batch, seq):
    return pl.pallas_call(
        functools.partial(_conf_body, seq=seq),
        out_shape=jax.ShapeDtypeStruct((batch * seq, D_CONF), _BF16),
        grid=(batch,),
        in_specs=[pl.BlockSpec((seq, D_CONF), lambda i: (i, 0)), _const_spec((CONF_K, D_CONF)),
                  _const_spec((1, D_CONF)), _const_spec((1, D_CONF)), _const_spec((1, D_CONF))],
        out_specs=pl.BlockSpec((seq, D_CONF), lambda i: (i, 0)),
        scratch_shapes=[pltpu.VMEM((seq + 5 * SUBLANES, D_CONF), _F32)],
        compiler_params=_params("arbitrary"),
        name="conf_prompt",
    )(u, w, b, g, lb)


def _conf_step_body(buf_ref, u_ref, w_ref, b_ref, g_ref, lb_ref, c_ref, nbuf_ref):
    w = w_ref[...]
    u = u_ref[...]
    conv = jnp.sum(buf_ref[...] * w[None, :CONF_K - 1, :], axis=1) + u * w[CONF_K - 1:CONF_K, :]
    c_ref[...] = _layernorm_silu(conv + b_ref[...], g_ref[...], lb_ref[...]).astype(_BF16)
    nbuf_ref[:, 0:CONF_K - 2, :] = buf_ref[:, 1:CONF_K - 1, :]
    nbuf_ref[:, CONF_K - 2:CONF_K - 1, :] = u[:, None, :]


def _conf_step(buf, u, w, b, g, lb):
    nb = buf.shape[0]
    bb = 16
    state = pl.BlockSpec((bb, CONF_K - 1, D_CONF), lambda i: (i, 0, 0))
    return pl.pallas_call(
        _conf_step_body,
        out_shape=(jax.ShapeDtypeStruct((nb, D_CONF), _BF16), jax.ShapeDtypeStruct(buf.shape, buf.dtype)),
        grid=(nb // bb,),
        in_specs=[state, pl.BlockSpec((bb, D_CONF), lambda i: (i, 0)), _const_spec((CONF_K, D_CONF)),
                  _const_spec((1, D_CONF)), _const_spec((1, D_CONF)), _const_spec((1, D_CONF))],
        out_specs=(pl.BlockSpec((bb, D_CONF), lambda i: (i, 0)), state),
        compiler_params=_params("arbitrary"),
        name="conf_step",
    )(buf, u, w, b, g, lb)


def _lru_gates(lx, wg_ref, bg_ref, lam_ref, head):
    sl = slice(head * HEAD, (head + 1) * HEAD)
    lxh = lx[:, sl]
    z = _dot(lxh.astype(_BF16), wg_ref[head]) + bg_ref[head:head + 1, :]
    r = jax.nn.sigmoid(z[:, :HEAD])
    i = jax.nn.sigmoid(z[:, HEAD:])
    log_a = -LRU_C * r * jax.nn.softplus(-lam_ref[:, sl])
    a = jnp.exp(log_a)
    return a, jnp.sqrt(-jnp.tanh(log_a) * (a * a + 1.0)) * i * lxh


def _lru_body(lx_ref, lg_ref, cw_ref, cb_ref, wg_ref, bg_ref, lam_ref, r_ref, hl_ref,
              halo_ref, h_ref, a_ref, u_ref, hs_ref):
    j = pl.program_id(1)
    tc = LRU_TC

    @pl.when(j == 0)
    def _():
        halo_ref[...] = jnp.zeros_like(halo_ref)
        h_ref[...] = jnp.zeros_like(h_ref)

    cur = lx_ref[...]
    ext = jnp.concatenate([halo_ref[...], cur], axis=0)
    lx = cb_ref[...] + cw_ref[LRU_K - 1:LRU_K, :] * cur
    for back in range(1, LRU_K):
        lx = lx + cw_ref[LRU_K - 1 - back:LRU_K - back, :] * ext[SUBLANES - back:SUBLANES - back + tc, :]
    halo_ref[...] = cur[tc - SUBLANES:, :]

    for head in range(N_HEADS):
        a, u = _lru_gates(lx, wg_ref, bg_ref, lam_ref, head)
        a_ref[head * LRU_PITCH:head * LRU_PITCH + tc, :] = a
        u_ref[head * LRU_PITCH:head * LRU_PITCH + tc, :] = u

    def step(t, h):
        rows = pl.ds(t, N_HEADS, stride=LRU_PITCH)
        h = a_ref[rows, :] * h + u_ref[rows, :]
        hs_ref[rows, :] = h
        return h

    h_ref[...] = lax.fori_loop(0, tc, step, h_ref[...], unroll=8)

    for head in range(N_HEADS):
        sl = slice(head * HEAD, (head + 1) * HEAD)
        hs = hs_ref[head * LRU_PITCH:head * LRU_PITCH + tc, :]
        r_ref[:, sl] = (hs * lg_ref[:, sl]).astype(_BF16)

    @pl.when(j == pl.num_programs(1) - 1)
    def _():
        hl_ref[0] = h_ref[...]


def _lru_prompt(lx, lg, cw, cb, wg, bg, lam, batch, seq):
    nchunk = seq // LRU_TC
    tok = pl.BlockSpec((LRU_TC, D_LRU), lambda b, j: (b * nchunk + j, 0))
    pitch_rows = N_HEADS * LRU_PITCH
    return pl.pallas_call(
        _lru_body,
        out_shape=(jax.ShapeDtypeStruct((batch * seq, D_LRU), _BF16),
                   jax.ShapeDtypeStruct((batch, N_HEADS, HEAD), _F32)),
        grid=(batch, nchunk),
        in_specs=[tok, tok, _const_spec((LRU_K, D_LRU)), _const_spec((1, D_LRU)),
                  _const_spec((N_HEADS, HEAD, 2 * HEAD)), _const_spec((N_HEADS, 2 * HEAD)),
                  _const_spec((1, D_LRU))],
        out_specs=(tok, pl.BlockSpec((1, N_HEADS, HEAD), lambda b, j: (b, 0, 0))),
        scratch_shapes=[pltpu.VMEM((SUBLANES, D_LRU), _F32), pltpu.VMEM((N_HEADS, HEAD), _F32),
                        pltpu.VMEM((pitch_rows, HEAD), _F32), pltpu.VMEM((pitch_rows, HEAD), _F32),
                        pltpu.VMEM((pitch_rows, HEAD), _F32)],
        compiler_params=_params("arbitrary", "arbitrary"),
        name="lru_prompt",
    )(lx, lg, cw, cb, wg, bg, lam)


def _lru_step_body(b0_ref, b1_ref, b2_ref, lx_ref, h0_ref, lg_ref, cw_ref, cb_ref, wg_ref, bg_ref, lam_ref,
                   r_ref, hn_ref):
    lx = (cb_ref[...] + cw_ref[0:1, :] * b0_ref[...] + cw_ref[1:2, :] * b1_ref[...]
          + cw_ref[2:3, :] * b2_ref[...] + cw_ref[3:4, :] * lx_ref[...])
    for head in range(N_HEADS):
        sl = slice(head * HEAD, (head + 1) * HEAD)
        a, u = _lru_gates(lx, wg_ref, bg_ref, lam_ref, head)
        h = a * h0_ref[:, sl] + u
        hn_ref[:, sl] = h
        r_ref[:, sl] = (h * lg_ref[:, sl]).astype(_BF16)


def _lru_step(b0, b1, b2, lx, h0, lg, cw, cb, wg, bg, lam):
    nb = lx.shape[0]
    full = _const_spec((nb, D_LRU))
    return pl.pallas_call(
        _lru_step_body,
        out_shape=(jax.ShapeDtypeStruct((nb, D_LRU), _BF16), jax.ShapeDtypeStruct((nb, D_LRU), _F32)),
        grid=(1,),
        in_specs=[full] * 6 + [_const_spec((LRU_K, D_LRU)), _const_spec((1, D_LRU)),
                               _const_spec((N_HEADS, HEAD, 2 * HEAD)), _const_spec((N_HEADS, 2 * HEAD)),
                               _const_spec((1, D_LRU))],
        out_specs=(pl.BlockSpec((nb, D_LRU), lambda i: (0, 0)), pl.BlockSpec((nb, D_LRU), lambda i: (0, 0))),
        compiler_params=_params("arbitrary"),
        name="lru_step",
    )(b0, b1, b2, lx, h0, lg, cw, cb, wg, bg, lam)


R_E1, R_E2, R_W1, R_W2, R_RANK1, R_RANK2 = range(6)


def _route_tile(logits, seen):
    m = logits.shape[0]
    lane = lax.broadcasted_iota(jnp.int32, (m, LANES), 1).astype(_F32)
    neg = jnp.float32(-jnp.inf)

    def first_max(vals):
        vmax = jnp.max(vals, axis=1, keepdims=True)
        return vmax, jnp.min(jnp.where(vals == vmax, lane, float(LANES)), axis=1, keepdims=True)

    in_groups = lane < N_GROUPS
    g_max, g_idx = first_max(jnp.where(in_groups, logits, neg))
    g_den = jnp.sum(jnp.where(in_groups, jnp.exp(logits - g_max), 0.0), axis=1, keepdims=True)
    lo = N_GROUPS + EPG * g_idx
    e_logits = jnp.where((lane >= lo) & (lane < lo + EPG), logits, neg)
    v1, i1 = first_max(e_logits)
    v2, i2 = first_max(jnp.where(lane == i1, neg, e_logits))
    t = jnp.exp(v2 - v1)
    w1 = 1.0 / (g_den * (1.0 + t))
    w2 = w1 * t
    e1 = i1 - N_GROUPS
    e2 = i2 - N_GROUPS

    is1 = lane == e1
    is2 = lane == e2
    onehot = jnp.where(is1 | is2, 1.0, 0.0)
    earlier = (lax.broadcasted_iota(jnp.int32, (m, m), 1) < lax.broadcasted_iota(jnp.int32, (m, m), 0))
    before = seen + _dot(jnp.where(earlier, 1.0, 0.0).astype(_BF16), onehot.astype(_BF16))
    rank1 = jnp.sum(jnp.where(is1, before, 0.0), axis=1, keepdims=True)
    rank2 = jnp.sum(jnp.where(is2, before, 0.0), axis=1, keepdims=True)

    rec = jnp.zeros((m, LANES), _F32)
    for pos, val in ((R_E1, e1), (R_E2, e2), (R_W1, w1), (R_W2, w2), (R_RANK1, rank1), (R_RANK2, rank2)):
        rec = jnp.where(lane == pos, val, rec)
    return rec, seen + jnp.sum(onehot, axis=0, keepdims=True)


def _outproj_body(cp_ref, rp_ref, xp_ref, cs_ref, rs_ref, xs_ref, wc_ref, wr_ref, g_ref, wrt_ref, brt_ref,
                  h1_ref, xn_ref, rec_ref, cnt_ref, seen_ref, *, n_ptiles, nb):
    i = pl.program_id(0)

    @pl.when(i == 0)
    def _():
        seen_ref[...] = jnp.zeros_like(seen_ref)

    def mix(c, r, x):
        h1 = x + _dot(c, wc_ref[...]) + _dot(r, wr_ref[...])
        xn = _rms(h1, g_ref[...])
        logits = _dot(xn.astype(_BF16), wrt_ref[...]) + brt_ref[...]
        rec, seen = _route_tile(logits, seen_ref[...])
        seen_ref[...] = seen
        return h1, xn, rec

    @pl.when(i < n_ptiles)
    def _():
        h1_ref[...], xn_ref[...], rec_ref[...] = mix(cp_ref[...], rp_ref[...], xp_ref[...])

    @pl.when(i == n_ptiles)
    def _():
        for ref, val in zip((h1_ref, xn_ref, rec_ref), mix(cs_ref[...], rs_ref[...], xs_ref[...])):
            ref[:nb, :] = val
            ref[nb:, :] = jnp.zeros((ref.shape[0] - nb, ref.shape[1]), _F32)

    cnt_ref[...] = jnp.broadcast_to(seen_ref[...], cnt_ref.shape)


def _outproj(cp, rp, xp, cs, rs, xs, wc, wr, g, wrt, brt, tm):
    n_ptiles = xp.shape[0] // tm
    nb = xs.shape[0]
    rows = (n_ptiles + 1) * tm
    last = n_ptiles - 1
    tok = lambda width: pl.BlockSpec((tm, width), lambda i: (jnp.minimum(i, last), 0))
    out_tok = lambda width: pl.BlockSpec((tm, width), lambda i: (i, 0))
    return pl.pallas_call(
        functools.partial(_outproj_body, n_ptiles=n_ptiles, nb=nb),
        out_shape=(jax.ShapeDtypeStruct((rows, D_MODEL), _F32), jax.ShapeDtypeStruct((rows, D_MODEL), _F32),
                   jax.ShapeDtypeStruct((rows, LANES), _F32), jax.ShapeDtypeStruct((SUBLANES, LANES), _F32)),
        grid=(n_ptiles + 1,),
        in_specs=[tok(D_CONF), tok(D_LRU), tok(D_MODEL), _const_spec((nb, D_CONF)), _const_spec((nb, D_LRU)),
                  _const_spec((nb, D_MODEL)), _const_spec((D_CONF, D_MODEL)), _const_spec((D_LRU, D_MODEL)),
                  _const_spec((1, D_MODEL)), _const_spec((D_MODEL, LANES)), _const_spec((1, LANES))],
        out_specs=(out_tok(D_MODEL), out_tok(D_MODEL), out_tok(LANES),
                   pl.BlockSpec((SUBLANES, LANES), lambda i: (0, 0))),
        scratch_shapes=[pltpu.VMEM((1, LANES), _F32)],
        compiler_params=_params("arbitrary"),
        name="outproj",
    )(cp, rp, xp, cs, rs, xs, wc, wr, g, wrt, brt)


def _dispatch_body(slot_ref, lo_ref, hi_ref, x_ref, xs_hbm, zeros, sem, zsem, *, tm, n_full, tail):
    i = pl.program_id(0)

    def zero_copies(n, wait):
        lo, hi = lo_ref[n], hi_ref[n]
        head_end = jnp.minimum(((lo + SUBLANES - 1) // SUBLANES) * SUBLANES, hi)

        def go(copy):
            copy.wait() if wait else copy.start()

        def single(s, carry):
            go(pltpu.make_async_copy(zeros.at[pl.ds(0, 1), :], xs_hbm.at[pl.ds(s, 1), :], zsem))
            return carry
        lax.fori_loop(lo, head_end, single, 0)

        size = hi - head_end
        n_bufs = size // TM_E

        def whole(c, carry):
            start = pl.multiple_of(head_end + c * TM_E, SUBLANES)
            go(pltpu.make_async_copy(zeros, xs_hbm.at[pl.ds(start, TM_E), :], zsem))
            return carry
        lax.fori_loop(0, n_bufs, whole, 0)
        start = pl.multiple_of(head_end + n_bufs * TM_E, SUBLANES)
        rest = pl.multiple_of(size - n_bufs * TM_E, SUBLANES)

        @pl.when(rest > 0)
        def _():
            go(pltpu.make_async_copy(zeros.at[pl.ds(0, rest), :], xs_hbm.at[pl.ds(start, rest), :], zsem))

    def scatter(rows):
        for r in range(rows):
            for k in range(2):
                s = slot_ref[(i * tm + r) * 2 + k]
                pltpu.make_async_copy(x_ref.at[pl.ds(r, 1), :], xs_hbm.at[pl.ds(s, 1), :], sem).start()
        for k in range(2):
            pltpu.make_async_copy(x_ref.at[pl.ds(0, rows), :], xs_hbm.at[pl.ds(0, rows), :], sem).wait()

    @pl.when(i == 0)
    def _():
        zeros[...] = jnp.zeros_like(zeros)
        for wait in (False, True):
            lax.fori_loop(0, N_EXPERTS + 1, lambda n, c: (zero_copies(n, wait), c)[1], 0)

    @pl.when(i < n_full)
    def _():
        scatter(tm)

    @pl.when(i == n_full)
    def _():
        scatter(tail)


def _dispatch(slots, pad_lo, pad_hi, xn, n_all, n_rows, tm):
    n_full, tail = divmod(n_all, tm)
    grid_spec = pltpu.PrefetchScalarGridSpec(
        num_scalar_prefetch=3,
        grid=(n_full + (1 if tail else 0),),
        in_specs=[pl.BlockSpec((tm, D_MODEL), lambda i, s, lo, hi: (i, 0))],
        out_specs=pl.BlockSpec(memory_space=pl.ANY),
        scratch_shapes=[pltpu.VMEM((TM_E, D_MODEL), _F32), pltpu.SemaphoreType.DMA(()),
                        pltpu.SemaphoreType.DMA(())],
    )
    return pl.pallas_call(
        functools.partial(_dispatch_body, tm=tm, n_full=n_full, tail=tail),
        out_shape=jax.ShapeDtypeStruct((n_rows, D_MODEL), _F32),
        grid_spec=grid_spec,
        compiler_params=_params("arbitrary"),
        name="dispatch",
    )(slots, pad_lo, pad_hi, xn)


def _moe_body(te_ref, nx_ref, nu_ref, x_ref, wg_hbm, wu_hbm, wd_hbm, y_ref,
              wg_f, wu_f, wd_f, wg_b, wu_b, wd_b, wsem):
    j = pl.program_id(0)
    n_used = nu_ref[0]

    def weight_copies(e):
        return (pltpu.make_async_copy(wg_hbm.at[e], wg_f, wsem.at[0]),
                pltpu.make_async_copy(wu_hbm.at[e], wu_f, wsem.at[1]),
                pltpu.make_async_copy(wd_hbm.at[e], wd_f, wsem.at[2]))

    @pl.when(j == 0)
    def _():
        for copy in weight_copies(te_ref[0]):
            copy.start()

    @pl.when(j < n_used)
    def _():
        @pl.when((j == 0) | (te_ref[j] != te_ref[jnp.maximum(j - 1, 0)]))
        def _():
            for copy in weight_copies(te_ref[j]):
                copy.wait()
            wg_b[...] = wg_f[...].astype(_BF16)
            wu_b[...] = wu_f[...].astype(_BF16)
            wd_b[...] = wd_f[...].astype(_BF16)

            @pl.when(nx_ref[j] >= 0)
            def _():
                for copy in weight_copies(nx_ref[j]):
                    copy.start()

        x = x_ref[...].astype(_BF16)
        g = _dot(x, wg_b[...])
        u = _dot(x, wu_b[...])
        hid = g * jax.nn.sigmoid(g) * u
        _store_token_major(y_ref, _dot(hid.astype(_BF16), wd_b[...]))

    @pl.when(j >= n_used)
    def _():
        y_ref[...] = jnp.zeros_like(y_ref)


def _moe(tile_expert, next_expert, n_used, xs, w_gate, w_up, w_down):
    n_tiles = tile_expert.shape[0]
    any_space = pl.BlockSpec(memory_space=pl.ANY)
    up_shape, down_shape = (D_MODEL, D_EXPERT), (D_EXPERT, D_MODEL)
    grid_spec = pltpu.PrefetchScalarGridSpec(
        num_scalar_prefetch=3,
        grid=(n_tiles,),
        in_specs=[pl.BlockSpec((TM_E, D_MODEL), lambda j, te, nx, nu: (jnp.minimum(j, nu[0] - 1), 0)),
                  any_space, any_space, any_space],
        out_specs=pl.BlockSpec((TM_E * TOK_ROWS, LANES), lambda j, te, nx, nu: (j, 0)),
        scratch_shapes=[pltpu.VMEM(up_shape, _F32), pltpu.VMEM(up_shape, _F32), pltpu.VMEM(down_shape, _F32),
                        pltpu.VMEM(up_shape, _BF16), pltpu.VMEM(up_shape, _BF16), pltpu.VMEM(down_shape, _BF16),
                        pltpu.SemaphoreType.DMA((3,))],
    )
    return pl.pallas_call(
        _moe_body,
        out_shape=jax.ShapeDtypeStruct((n_tiles * TM_E * TOK_ROWS, LANES), _F32),
        grid_spec=grid_spec,
        compiler_params=_params("arbitrary"),
        name="moe",
    )(tile_expert, next_expert, n_used, xs, w_gate, w_up, w_down)


def _ple_body(slot_ref, h1_ref, rec_ref, p_ref, g_ref, wpg_ref, bpg_ref, wple_ref, fg_ref, ys_hbm, o_ref,
              ybuf, sem, *, tm):
    i = pl.program_id(0)
    cur = lax.rem(i, 2)

    def start_gather(tile, buf):
        for r in range(tm):
            for k in range(2):
                row0 = pl.multiple_of(slot_ref[(tile * tm + r) * 2 + k] * TOK_ROWS, TOK_ROWS)
                pltpu.make_async_copy(ys_hbm.at[pl.ds(row0, TOK_ROWS), :],
                                      ybuf.at[buf, k, pl.ds(r * TOK_PITCH, TOK_ROWS), :], sem.at[buf]).start()

    def wait_gather(buf):
        rows = pl.ds(0, tm * TOK_ROWS)
        for k in range(2):
            pltpu.make_async_copy(ys_hbm.at[rows, :], ybuf.at[buf, k, rows, :], sem.at[buf]).wait()

    def finish():
        rec = rec_ref[...]
        y1, y2 = (_load_token_major(lambda rows, k=k: ybuf[cur, k, rows, :], tm, TOK_PITCH) for k in range(2))
        h2 = h1_ref[...] + rec[:, R_W1:R_W1 + 1] * y1 + rec[:, R_W2:R_W2 + 1] * y2
        hn = _rms(h2, g_ref[...]).astype(_BF16)
        gate = jax.nn.sigmoid(_dot(hn, wpg_ref[...]) + bpg_ref[...])
        pe = _dot(p_ref[...].astype(_BF16), wple_ref[...])
        o_ref[...] = _rms(h2 + gate * pe, fg_ref[...])

    @pl.when(i == 0)
    def _():
        start_gather(0, 0)

    wait_gather(cur)

    @pl.when(i + 1 < pl.num_programs(0))
    def _():
        finish()
        start_gather(i + 1, 1 - cur)

    @pl.when(i + 1 == pl.num_programs(0))
    def _():
        finish()


def _ple(h1, ys, slots, rec, p, g, wpg, bpg, wple, fg, tm, row0):
    n = p.shape[0]
    off = row0 // tm
    tok = lambda width: pl.BlockSpec((tm, width), lambda i, s: (i + off, 0))
    const = lambda shape: pl.BlockSpec(shape, lambda i, s: (0,) * len(shape), pipeline_mode=pl.Buffered(1))
    grid_spec = pltpu.PrefetchScalarGridSpec(
        num_scalar_prefetch=1,
        grid=(n // tm,),
        in_specs=[tok(D_MODEL), tok(LANES), pl.BlockSpec((tm, D_PLE), lambda i, s: (i, 0)),
                  const((1, D_MODEL)), const((D_MODEL, D_MODEL)), const((1, D_MODEL)),
                  const((D_PLE, D_MODEL)), const((1, D_MODEL)), pl.BlockSpec(memory_space=pl.ANY)],
        out_specs=pl.BlockSpec((tm, D_MODEL), lambda i, s: (i, 0)),
        scratch_shapes=[pltpu.VMEM((2, 2, tm * TOK_PITCH, LANES), _F32), pltpu.SemaphoreType.DMA((2,))],
    )
    return pl.pallas_call(
        functools.partial(_ple_body, tm=tm),
        out_shape=jax.ShapeDtypeStruct((n, D_MODEL), _F32),
        grid_spec=grid_spec,
        compiler_params=_params("arbitrary"),
        name="ple",
    )(slots, h1, rec, p, g, wpg, bpg, wple, fg, ys)


def _schedule(rec, counts_row, n_all, n_tiles):
    ids = jnp.arange(N_EXPERTS, dtype=jnp.int32)
    expert = rec[:n_all, R_E1:R_E2 + 1].astype(jnp.int32)
    rank = rec[:n_all, R_RANK1:R_RANK2 + 1].astype(jnp.int32)
    counts = counts_row[0, :N_EXPERTS].astype(jnp.int32)
    tiles_per = (counts + TM_E - 1) // TM_E
    tiles_end = jnp.cumsum(tiles_per)
    n_used = tiles_end[-1]
    first_row = (tiles_end - tiles_per) * TM_E
    slot = jnp.sum(jnp.where(expert[..., None] == ids, first_row, 0), axis=-1) + rank
    pad_lo = jnp.concatenate([first_row + counts, (n_used * TM_E).reshape(1)]).astype(jnp.int32)
    pad_hi = jnp.concatenate([tiles_end * TM_E, jnp.full((1,), n_tiles * TM_E)]).astype(jnp.int32)
    tile_ids = jnp.arange(n_tiles, dtype=jnp.int32)
    te = jnp.minimum(jnp.sum((tiles_end[None, :] <= tile_ids[:, None]).astype(jnp.int32), axis=1), N_EXPERTS - 1)
    last_expert = jnp.sum(jnp.where(tile_ids == n_used - 1, te, 0))
    te = jnp.where(tile_ids < n_used, te, last_expert).astype(jnp.int32)
    later = (ids[None, :] > te[:, None]) & (counts[None, :] > 0)
    nxt = jnp.min(jnp.where(later, ids[None, :], N_EXPERTS), axis=1)
    nxt = jnp.where(nxt < N_EXPERTS, nxt, -1).astype(jnp.int32)
    return te, nxt, n_used.astype(jnp.int32).reshape(1), slot.reshape(-1), pad_lo, pad_hi


def kernel(x_prompt, x_sample, state_conf_conv, state_lru_conv, state_lru_h, p_prompt, p_sample, norm1_g, w_in, b_in, conf_dw_w, conf_dw_b, conf_ln_g, conf_ln_b, lru_conv_w, lru_conv_b, lru_wa, lru_ba, lru_wx, lru_bx, lru_lambda, w_out, norm2_g, w_grp, b_grp, w_rt, b_rt, w_gate, w_up, w_down, ple_norm_g, w_ple, w_pg, b_pg, final_g):
    batch, seq, _ = x_prompt.shape
    nb = x_sample.shape[0]
    n_p = batch * seq
    n_all = n_p + nb
    layer = 0

    row = lambda v: v.reshape(1, -1)
    w_in_b = w_in[layer].astype(_BF16)
    wc_b = w_out[layer, :D_CONF].astype(_BF16)
    wr_b = w_out[layer, D_CONF:].astype(_BF16)
    w_router = jnp.concatenate(
        [w_grp[layer], w_rt[layer], jnp.zeros((D_MODEL, LANES - N_GROUPS - N_EXPERTS), _F32)], axis=1).astype(_BF16)
    b_router = jnp.concatenate([b_grp[layer], b_rt[layer], jnp.zeros((LANES - N_GROUPS - N_EXPERTS,), _F32)])
    w_gates = jnp.concatenate([lru_wa[layer], lru_wx[layer]], axis=-1).astype(_BF16)
    b_gates = jnp.concatenate([lru_ba[layer], lru_bx[layer]], axis=-1)
    w_pg_b = w_pg[layer].astype(_BF16)
    w_ple_b = w_ple[layer].astype(_BF16)
    conf_args = (conf_dw_w[layer], row(conf_dw_b[layer]), row(conf_ln_g[layer]), row(conf_ln_b[layer]))
    lru_args = (lru_conv_w[layer], row(lru_conv_b[layer]), w_gates, b_gates, row(lru_lambda[layer]))
    out_args = (wc_b, wr_b, row(norm2_g[layer]), w_router, row(b_router))
    ple_args = (row(ple_norm_g[layer]), w_pg_b, row(b_pg[layer]), w_ple_b, row(final_g))

    xp = x_prompt.reshape(n_p, D_MODEL)
    u_p, lx_p, lg_p = _inproj(xp, row(norm1_g[layer]), w_in_b, row(b_in[layer]), TM)
    c_p = _conf_prompt(u_p, *conf_args, batch, seq)
    r_p, hl_p = _lru_prompt(lx_p, lg_p, *lru_args, batch, seq)

    xs = x_sample.reshape(nb, D_MODEL)
    conf_buf = state_conf_conv[layer]
    lru_buf = state_lru_conv[layer]
    u_s, lx_s, lg_s = _inproj(xs, row(norm1_g[layer]), w_in_b, row(b_in[layer]), nb)
    c_s, conf_buf_new = _conf_step(conf_buf, u_s, *conf_args)
    r_s, hn_s = _lru_step(lru_buf[:, 0], lru_buf[:, 1], lru_buf[:, 2], lx_s, state_lru_h[layer], lg_s, *lru_args)
    h1, xn, rec, counts = _outproj(c_p, r_p, xp, c_s, r_s, xs, *out_args, TM)

    n_tiles = (2 * n_all + TM_E - 1) // TM_E + N_EXPERTS
    tile_expert, next_expert, n_used, slots, pad_lo, pad_hi = _schedule(rec, counts, n_all, n_tiles)
    xs = _dispatch(slots, pad_lo, pad_hi, xn, n_all, n_tiles * TM_E, TM)
    ys = _moe(tile_expert, next_expert, n_used, xs, w_gate[layer], w_up[layer], w_down[layer])

    y_p = _ple(h1, ys, slots[:2 * n_p], rec, p_prompt[layer].reshape(n_p, D_PLE), *ple_args, TM_PLE, 0)
    y_s = _ple(h1, ys, slots[2 * n_p:], rec, p_sample[layer].reshape(nb, D_PLE), *ple_args, nb, n_p)

    u_p3 = u_p.reshape(batch, seq, D_CONF)
    lx_p3 = lx_p.reshape(batch, seq, D_LRU)
    return (
        y_p.reshape(batch, seq, D_MODEL),
        y_s.reshape(nb, 1, D_MODEL),
        u_p3[None, :, seq - (CONF_K - 1):],
        conf_buf_new[None],
        lx_p3[None, :, seq - (LRU_K - 1):],
        jnp.concatenate([lru_buf[:, 1:], lx_s[:, None]], axis=1)[None],
        hl_p.reshape(1, batch, D_LRU),
        hn_s[None],
    )
```

```python
import functools

import jax
import jax.numpy as jnp
from jax import lax
from jax.experimental import pallas as pl
from jax.experimental.pallas import tpu as pltpu

D_MODEL = 2048
D_CONF = 1024
D_LRU = 1024
N_HEADS = 8
HEAD = 128
CONF_K = 31
LRU_K = 4
LRU_C = 8.0
N_GROUPS = 4
EPG = 8
N_EXPERTS = 32
D_EXPERT = 512
D_PLE = 256
EPS = 1e-6

LANES = 128
SUBLANES = 8
VMEM_LIMIT = 56 * 1024 * 1024

TM = 512
TM_E = 256
CONF_TC = 128
LRU_TC = 512
LRU_PITCH = LRU_TC + SUBLANES
TM_PLE = 256
TOK_ROWS = D_MODEL // LANES
TOK_PITCH = TOK_ROWS + SUBLANES

_BF16 = jnp.bfloat16
_F32 = jnp.float32


def _params(*sem):
    return pltpu.CompilerParams(dimension_semantics=sem, vmem_limit_bytes=VMEM_LIMIT)


def _const_spec(shape):
    nd = len(shape)
    return pl.BlockSpec(shape, lambda *_: (0,) * nd, pipeline_mode=pl.Buffered(1))


def _store_token_major(ref, val, pitch=TOK_ROWS):
    for tb in range(val.shape[0] // SUBLANES):
        for s in range(TOK_ROWS):
            ref[pl.ds(tb * SUBLANES * pitch + s, SUBLANES, stride=pitch), :] = (
                val[tb * SUBLANES:(tb + 1) * SUBLANES, s * LANES:(s + 1) * LANES])


def _load_token_major(load, m, pitch):
    slabs = [jnp.concatenate([load(pl.ds(tb * SUBLANES * pitch + s, SUBLANES, stride=pitch))
                              for tb in range(m // SUBLANES)], axis=0) for s in range(TOK_ROWS)]
    return jnp.concatenate(slabs, axis=1)


def _rms(x, g):
    return x * lax.rsqrt(jnp.mean(x * x, axis=-1, keepdims=True) + EPS) * g


def _dot(a, b):
    return jnp.dot(a, b, preferred_element_type=_F32)


def _inproj_body(x_ref, g_ref, w_ref, b_ref, u_ref, lx_ref, lg_ref):
    xn = _rms(x_ref[...], g_ref[...]).astype(_BF16)

    def proj(k):
        sl = slice(k * D_CONF, (k + 1) * D_CONF)
        return _dot(xn, w_ref[:, sl]) + b_ref[:, sl]

    u_ref[...] = proj(0) * jax.nn.sigmoid(proj(1))
    lx_ref[...] = proj(2)
    lg_ref[...] = jax.nn.gelu(proj(3))


def _inproj(x, g, w, b, tm):
    n = x.shape[0]
    out = jax.ShapeDtypeStruct((n, D_CONF), _F32)
    tok = lambda width: pl.BlockSpec((tm, width), lambda i: (i, 0))
    return pl.pallas_call(
        _inproj_body,
        out_shape=(out, out, out),
        grid=(n // tm,),
        in_specs=[tok(D_MODEL), _const_spec((1, D_MODEL)), _const_spec((D_MODEL, 4 * D_CONF)),
                  _const_spec((1, 4 * D_CONF))],
        out_specs=(tok(D_CONF), tok(D_CONF), tok(D_CONF)),
        compiler_params=_params("arbitrary"),
        name="inproj",
    )(x, g, w, b)


def _layernorm_silu(conv, g, b):
    mu = jnp.mean(conv, axis=-1, keepdims=True)
    cen = conv - mu
    var = jnp.mean(cen * cen, axis=-1, keepdims=True)
    y = cen * lax.rsqrt(var + EPS) * g + b
    return y * jax.nn.sigmoid(y)


def _conf_body(u_ref, w_ref, b_ref, g_ref, lb_ref, c_ref, s_ref, *, seq):
    head = 4 * SUBLANES
    s_ref[0:head, :] = jnp.zeros((head, D_CONF), _F32)
    s_ref[head:head + seq, :] = u_ref[...]
    s_ref[head + seq:, :] = jnp.zeros((SUBLANES, D_CONF), _F32)
    tc = CONF_TC

    def chunk(ci, carry):
        t0 = pl.multiple_of(ci * tc, tc)
        win = s_ref[pl.ds(t0, tc + 5 * SUBLANES), :]
        out = None
        for r in range(SUBLANES):
            part = None
            for q in range(5):
                m = SUBLANES * q + r
                if 2 <= m <= CONF_K + 1:
                    term = win[SUBLANES * q:SUBLANES * q + tc + SUBLANES, :] * w_ref[m - 2:m - 1, :]
                    part = term if part is None else part + term
            shifted = part[r:r + tc, :]
            out = shifted if out is None else out + shifted
        c = _layernorm_silu(out + b_ref[...], g_ref[...], lb_ref[...])
        c_ref[pl.ds(t0, tc), :] = c.astype(_BF16)
        return carry

    lax.fori_loop(0, seq // tc, chunk, 0)


def _conf_prompt(u, w, b, g, lb, batch, seq):
    return pl.pallas_call(
        functools.partial(_conf_body, seq=seq),
        out_shape=jax.ShapeDtypeStruct((batch * seq, D_CONF), _BF16),
        grid=(batch,),
        in_specs=[pl.BlockSpec((seq, D_CONF), lambda i: (i, 0)), _const_spec((CONF_K, D_CONF)),
                  _const_spec((1, D_CONF)), _const_spec((1, D_CONF)), _const_spec((1, D_CONF))],
        out_specs=pl.BlockSpec((seq, D_CONF), lambda i: (i, 0)),
        scratch_shapes=[pltpu.VMEM((seq + 5 * SUBLANES, D_CONF), _F32)],
        compiler_params=_params("arbitrary"),
        name="conf_prompt",
    )(u, w, b, g, lb)


def _conf_step_body(buf_ref, u_ref, w_ref, b_ref, g_ref, lb_ref, c_ref, nbuf_ref):
    u = u_ref[...]
    conv = u * w_ref[CONF_K - 1:CONF_K, :]
    for k in range(CONF_K - 1):
        conv = conv + buf_ref[k] * w_ref[k:k + 1, :]
    c_ref[...] = _layernorm_silu(conv + b_ref[...], g_ref[...], lb_ref[...]).astype(_BF16)
    for k in range(CONF_K - 2):
        nbuf_ref[k] = buf_ref[k + 1]
    nbuf_ref[CONF_K - 2] = u


def _conf_step(buf, layer, u, w, b, g, lb):
    depth, _, nb, _ = buf.shape
    assert depth == 1
    bb = 16
    state = pl.BlockSpec((None, CONF_K - 1, bb, D_CONF), lambda i: (layer, 0, i, 0))
    return pl.pallas_call(
        _conf_step_body,
        out_shape=(jax.ShapeDtypeStruct((nb, D_CONF), _BF16), jax.ShapeDtypeStruct(buf.shape, buf.dtype)),
        grid=(nb // bb,),
        in_specs=[state, pl.BlockSpec((bb, D_CONF), lambda i: (i, 0)), _const_spec((CONF_K, D_CONF)),
                  _const_spec((1, D_CONF)), _const_spec((1, D_CONF)), _const_spec((1, D_CONF))],
        out_specs=(pl.BlockSpec((bb, D_CONF), lambda i: (i, 0)), state),
        compiler_params=_params("arbitrary"),
        name="conf_step",
    )(buf, u, w, b, g, lb)


def _lru_gates(lx, wg_ref, bg_ref, lam_ref, head):
    sl = slice(head * HEAD, (head + 1) * HEAD)
    lxh = lx[:, sl]
    z = _dot(lxh.astype(_BF16), wg_ref[head]) + bg_ref[head:head + 1, :]
    r = jax.nn.sigmoid(z[:, :HEAD])
    i = jax.nn.sigmoid(z[:, HEAD:])
    log_a = -LRU_C * r * jax.nn.softplus(-lam_ref[:, sl])
    a = jnp.exp(log_a)
    return a, jnp.sqrt(-jnp.tanh(log_a) * (a * a + 1.0)) * i * lxh


def _lru_body(lx_ref, lg_ref, cw_ref, cb_ref, wg_ref, bg_ref, lam_ref, r_ref, hl_ref,
              halo_ref, h_ref, a_ref, u_ref, hs_ref):
    j = pl.program_id(1)
    tc = LRU_TC

    @pl.when(j == 0)
    def _():
        halo_ref[...] = jnp.zeros_like(halo_ref)
        h_ref[...] = jnp.zeros_like(h_ref)

    cur = lx_ref[...]
    ext = jnp.concatenate([halo_ref[...], cur], axis=0)
    lx = cb_ref[...] + cw_ref[LRU_K - 1:LRU_K, :] * cur
    for back in range(1, LRU_K):
        lx = lx + cw_ref[LRU_K - 1 - back:LRU_K - back, :] * ext[SUBLANES - back:SUBLANES - back + tc, :]
    halo_ref[...] = cur[tc - SUBLANES:, :]

    for head in range(N_HEADS):
        a, u = _lru_gates(lx, wg_ref, bg_ref, lam_ref, head)
        a_ref[head * LRU_PITCH:head * LRU_PITCH + tc, :] = a
        u_ref[head * LRU_PITCH:head * LRU_PITCH + tc, :] = u

    def step(t, h):
        rows = pl.ds(t, N_HEADS, stride=LRU_PITCH)
        h = a_ref[rows, :] * h + u_ref[rows, :]
        hs_ref[rows, :] = h
        return h

    h_ref[...] = lax.fori_loop(0, tc, step, h_ref[...], unroll=8)

    for head in range(N_HEADS):
        sl = slice(head * HEAD, (head + 1) * HEAD)
        hs = hs_ref[head * LRU_PITCH:head * LRU_PITCH + tc, :]
        r_ref[:, sl] = (hs * lg_ref[:, sl]).astype(_BF16)

    @pl.when(j == pl.num_programs(1) - 1)
    def _():
        hl_ref[0] = h_ref[...]


def _lru_prompt(lx, lg, cw, cb, wg, bg, lam, batch, seq):
    nchunk = seq // LRU_TC
    tok = pl.BlockSpec((LRU_TC, D_LRU), lambda b, j: (b * nchunk + j, 0))
    pitch_rows = N_HEADS * LRU_PITCH
    return pl.pallas_call(
        _lru_body,
        out_shape=(jax.ShapeDtypeStruct((batch * seq, D_LRU), _BF16),
                   jax.ShapeDtypeStruct((batch, N_HEADS, HEAD), _F32)),
        grid=(batch, nchunk),
        in_specs=[tok, tok, _const_spec((LRU_K, D_LRU)), _const_spec((1, D_LRU)),
                  _const_spec((N_HEADS, HEAD, 2 * HEAD)), _const_spec((N_HEADS, 2 * HEAD)),
                  _const_spec((1, D_LRU))],
        out_specs=(tok, pl.BlockSpec((1, N_HEADS, HEAD), lambda b, j: (b, 0, 0))),
        scratch_shapes=[pltpu.VMEM((SUBLANES, D_LRU), _F32), pltpu.VMEM((N_HEADS, HEAD), _F32),
                        pltpu.VMEM((pitch_rows, HEAD), _F32), pltpu.VMEM((pitch_rows, HEAD), _F32),
                        pltpu.VMEM((pitch_rows, HEAD), _F32)],
        compiler_params=_params("arbitrary", "arbitrary"),
        name="lru_prompt",
    )(lx, lg, cw, cb, wg, bg, lam)


def _lru_step_body(buf_ref, lx_ref, h0_ref, lg_ref, cw_ref, cb_ref, wg_ref, bg_ref, lam_ref,
                   r_ref, hn_ref, nbuf_ref):
    cur = lx_ref[...]
    lx = cb_ref[...] + cw_ref[LRU_K - 1:LRU_K, :] * cur
    for k in range(LRU_K - 1):
        lx = lx + cw_ref[k:k + 1, :] * buf_ref[:, k, :]
    for k in range(LRU_K - 2):
        nbuf_ref[:, k, :] = buf_ref[:, k + 1, :]
    nbuf_ref[:, LRU_K - 2, :] = cur
    for head in range(N_HEADS):
        sl = slice(head * HEAD, (head + 1) * HEAD)
        a, u = _lru_gates(lx, wg_ref, bg_ref, lam_ref, head)
        h = a * h0_ref[:, sl] + u
        hn_ref[:, sl] = h
        r_ref[:, sl] = (h * lg_ref[:, sl]).astype(_BF16)


def _lru_step(buf, layer, lx, h0, lg, cw, cb, wg, bg, lam):
    depth, nb = buf.shape[:2]
    assert depth == 1
    full = _const_spec((nb, D_LRU))
    state = pl.BlockSpec((None, nb, LRU_K - 1, D_LRU), lambda i: (layer, 0, 0, 0))
    tok = pl.BlockSpec((nb, D_LRU), lambda i: (0, 0))
    return pl.pallas_call(
        _lru_step_body,
        out_shape=(jax.ShapeDtypeStruct((nb, D_LRU), _BF16), jax.ShapeDtypeStruct((nb, D_LRU), _F32),
                   jax.ShapeDtypeStruct(buf.shape, buf.dtype)),
        grid=(1,),
        in_specs=[state, full, full, full, _const_spec((LRU_K, D_LRU)), _const_spec((1, D_LRU)),
                  _const_spec((N_HEADS, HEAD, 2 * HEAD)), _const_spec((N_HEADS, 2 * HEAD)),
                  _const_spec((1, D_LRU))],
        out_specs=(tok, tok, state),
        compiler_params=_params("arbitrary"),
        name="lru_step",
    )(buf, lx, h0, lg, cw, cb, wg, bg, lam)


R_E1, R_E2, R_W1, R_W2, R_RANK1, R_RANK2 = range(6)


def _route_tile(logits, seen):
    m = logits.shape[0]
    lane = lax.broadcasted_iota(jnp.int32, (m, LANES), 1).astype(_F32)
    neg = jnp.float32(-jnp.inf)

    def first_max(vals):
        vmax = jnp.max(vals, axis=1, keepdims=True)
        return vmax, jnp.min(jnp.where(vals == vmax, lane, float(LANES)), axis=1, keepdims=True)

    in_groups = lane < N_GROUPS
    g_max, g_idx = first_max(jnp.where(in_groups, logits, neg))
    g_den = jnp.sum(jnp.where(in_groups, jnp.exp(logits - g_max), 0.0), axis=1, keepdims=True)
    lo = N_GROUPS + EPG * g_idx
    e_logits = jnp.where((lane >= lo) & (lane < lo + EPG), logits, neg)
    v1, i1 = first_max(e_logits)
    v2, i2 = first_max(jnp.where(lane == i1, neg, e_logits))
    t = jnp.exp(v2 - v1)
    w1 = 1.0 / (g_den * (1.0 + t))
    w2 = w1 * t
    e1 = i1 - N_GROUPS
    e2 = i2 - N_GROUPS

    is1 = lane == e1
    is2 = lane == e2
    onehot = jnp.where(is1 | is2, 1.0, 0.0)
    earlier = (lax.broadcasted_iota(jnp.int32, (m, m), 1) < lax.broadcasted_iota(jnp.int32, (m, m), 0))
    before = seen + _dot(jnp.where(earlier, 1.0, 0.0).astype(_BF16), onehot.astype(_BF16))
    rank1 = jnp.sum(jnp.where(is1, before, 0.0), axis=1, keepdims=True)
    rank2 = jnp.sum(jnp.where(is2, before, 0.0), axis=1, keepdims=True)

    rec = jnp.zeros((m, LANES), _F32)
    for pos, val in ((R_E1, e1), (R_E2, e2), (R_W1, w1), (R_W2, w2), (R_RANK1, rank1), (R_RANK2, rank2)):
        rec = jnp.where(lane == pos, val, rec)
    return rec, seen + jnp.sum(onehot, axis=0, keepdims=True)


def _outproj_body(cp_ref, rp_ref, xp_ref, cs_ref, rs_ref, xs_ref, wc_ref, wr_ref, g_ref, wrt_ref, brt_ref,
                  h1_ref, xn_ref, rec_ref, cnt_ref, seen_ref, *, n_ptiles, nb):
    i = pl.program_id(0)

    @pl.when(i == 0)
    def _():
        seen_ref[...] = jnp.zeros_like(seen_ref)

    def mix(c, r, x):
        h1 = x + _dot(c, wc_ref[...]) + _dot(r, wr_ref[...])
        xn = _rms(h1, g_ref[...])
        logits = _dot(xn.astype(_BF16), wrt_ref[...]) + brt_ref[...]
        rec, seen = _route_tile(logits, seen_ref[...])
        seen_ref[...] = seen
        return h1, xn, rec

    @pl.when(i < n_ptiles)
    def _():
        h1_ref[...], xn_ref[...], rec_ref[...] = mix(cp_ref[...], rp_ref[...], xp_ref[...])

    @pl.when(i == n_ptiles)
    def _():
        for ref, val in zip((h1_ref, xn_ref, rec_ref), mix(cs_ref[...], rs_ref[...], xs_ref[...])):
            ref[:nb, :] = val
            ref[nb:, :] = jnp.zeros((ref.shape[0] - nb, ref.shape[1]), _F32)

    cnt_ref[...] = jnp.broadcast_to(seen_ref[...], cnt_ref.shape)


def _outproj(cp, rp, xp, cs, rs, xs, wc, wr, g, wrt, brt, tm):
    n_ptiles = xp.shape[0] // tm
    nb = xs.shape[0]
    rows = (n_ptiles + 1) * tm
    last = n_ptiles - 1
    tok = lambda width: pl.BlockSpec((tm, width), lambda i: (jnp.minimum(i, last), 0))
    out_tok = lambda width: pl.BlockSpec((tm, width), lambda i: (i, 0))
    return pl.pallas_call(
        functools.partial(_outproj_body, n_ptiles=n_ptiles, nb=nb),
        out_shape=(jax.ShapeDtypeStruct((rows, D_MODEL), _F32), jax.ShapeDtypeStruct((rows, D_MODEL), _F32),
                   jax.ShapeDtypeStruct((rows, LANES), _F32), jax.ShapeDtypeStruct((SUBLANES, LANES), _F32)),
        grid=(n_ptiles + 1,),
        in_specs=[tok(D_CONF), tok(D_LRU), tok(D_MODEL), _const_spec((nb, D_CONF)), _const_spec((nb, D_LRU)),
                  _const_spec((nb, D_MODEL)), _const_spec((D_CONF, D_MODEL)), _const_spec((D_LRU, D_MODEL)),
                  _const_spec((1, D_MODEL)), _const_spec((D_MODEL, LANES)), _const_spec((1, LANES))],
        out_specs=(out_tok(D_MODEL), out_tok(D_MODEL), out_tok(LANES),
                   pl.BlockSpec((SUBLANES, LANES), lambda i: (0, 0))),
        scratch_shapes=[pltpu.VMEM((1, LANES), _F32)],
        compiler_params=_params("arbitrary"),
        name="outproj",
    )(cp, rp, xp, cs, rs, xs, wc, wr, g, wrt, brt)


def _dispatch_body(slot_ref, lo_ref, hi_ref, x_ref, xs_hbm, zeros, sem, zsem, *, tm, n_full, tail):
    i = pl.program_id(0)

    def zero_copies(n, wait):
        lo, hi = lo_ref[n], hi_ref[n]
        head_end = jnp.minimum(((lo + SUBLANES - 1) // SUBLANES) * SUBLANES, hi)

        def go(copy):
            copy.wait() if wait else copy.start()

        def single(s, carry):
            go(pltpu.make_async_copy(zeros.at[pl.ds(0, 1), :], xs_hbm.at[pl.ds(s, 1), :], zsem))
            return carry
        lax.fori_loop(lo, head_end, single, 0)

        size = hi - head_end
        n_bufs = size // TM_E

        def whole(c, carry):
            start = pl.multiple_of(head_end + c * TM_E, SUBLANES)
            go(pltpu.make_async_copy(zeros, xs_hbm.at[pl.ds(start, TM_E), :], zsem))
            return carry
        lax.fori_loop(0, n_bufs, whole, 0)
        start = pl.multiple_of(head_end + n_bufs * TM_E, SUBLANES)
        rest = pl.multiple_of(size - n_bufs * TM_E, SUBLANES)

        @pl.when(rest > 0)
        def _():
            go(pltpu.make_async_copy(zeros.at[pl.ds(0, rest), :], xs_hbm.at[pl.ds(start, rest), :], zsem))

    def scatter(rows):
        for r in range(rows):
            for k in range(2):
                s = slot_ref[(i * tm + r) * 2 + k]
                pltpu.make_async_copy(x_ref.at[pl.ds(r, 1), :], xs_hbm.at[pl.ds(s, 1), :], sem).start()
        for k in range(2):
            pltpu.make_async_copy(x_ref.at[pl.ds(0, rows), :], xs_hbm.at[pl.ds(0, rows), :], sem).wait()

    @pl.when(i == 0)
    def _():
        zeros[...] = jnp.zeros_like(zeros)
        for wait in (False, True):
            lax.fori_loop(0, N_EXPERTS + 1, lambda n, c: (zero_copies(n, wait), c)[1], 0)

    @pl.when(i < n_full)
    def _():
        scatter(tm)

    @pl.when(i == n_full)
    def _():
        scatter(tail)


def _dispatch(slots, pad_lo, pad_hi, xn, n_all, n_rows, tm):
    n_full, tail = divmod(n_all, tm)
    grid_spec = pltpu.PrefetchScalarGridSpec(
        num_scalar_prefetch=3,
        grid=(n_full + (1 if tail else 0),),
        in_specs=[pl.BlockSpec((tm, D_MODEL), lambda i, s, lo, hi: (i, 0))],
        out_specs=pl.BlockSpec(memory_space=pl.ANY),
        scratch_shapes=[pltpu.VMEM((TM_E, D_MODEL), _F32), pltpu.SemaphoreType.DMA(()),
                        pltpu.SemaphoreType.DMA(())],
    )
    return pl.pallas_call(
        functools.partial(_dispatch_body, tm=tm, n_full=n_full, tail=tail),
        out_shape=jax.ShapeDtypeStruct((n_rows, D_MODEL), _F32),
        grid_spec=grid_spec,
        compiler_params=_params("arbitrary"),
        name="dispatch",
    )(slots, pad_lo, pad_hi, xn)


def _moe_body(te_ref, nx_ref, nu_ref, x_ref, wg_hbm, wu_hbm, wd_hbm, y_ref,
              wg_f, wu_f, wd_f, wg_b, wu_b, wd_b, wsem):
    j = pl.program_id(0)
    n_used = nu_ref[0]

    def weight_copies(e):
        return (pltpu.make_async_copy(wg_hbm.at[e], wg_f, wsem.at[0]),
                pltpu.make_async_copy(wu_hbm.at[e], wu_f, wsem.at[1]),
                pltpu.make_async_copy(wd_hbm.at[e], wd_f, wsem.at[2]))

    @pl.when(j == 0)
    def _():
        for copy in weight_copies(te_ref[0]):
            copy.start()

    @pl.when(j < n_used)
    def _():
        @pl.when((j == 0) | (te_ref[j] != te_ref[jnp.maximum(j - 1, 0)]))
        def _():
            for copy in weight_copies(te_ref[j]):
                copy.wait()
            wg_b[...] = wg_f[...].astype(_BF16)
            wu_b[...] = wu_f[...].astype(_BF16)
            wd_b[...] = wd_f[...].astype(_BF16)

            @pl.when(nx_ref[j] >= 0)
            def _():
                for copy in weight_copies(nx_ref[j]):
                    copy.start()

        x = x_ref[...].astype(_BF16)
        g = _dot(x, wg_b[...])
        u = _dot(x, wu_b[...])
        hid = g * jax.nn.sigmoid(g) * u
        _store_token_major(y_ref, _dot(hid.astype(_BF16), wd_b[...]))

    @pl.when(j >= n_used)
    def _():
        y_ref[...] = jnp.zeros_like(y_ref)


def _moe(tile_expert, next_expert, n_used, xs, w_gate, w_up, w_down):
    n_tiles = tile_expert.shape[0]
    any_space = pl.BlockSpec(memory_space=pl.ANY)
    up_shape, down_shape = (D_MODEL, D_EXPERT), (D_EXPERT, D_MODEL)
    grid_spec = pltpu.PrefetchScalarGridSpec(
        num_scalar_prefetch=3,
        grid=(n_tiles,),
        in_specs=[pl.BlockSpec((TM_E, D_MODEL), lambda j, te, nx, nu: (jnp.minimum(j, nu[0] - 1), 0)),
                  any_space, any_space, any_space],
        out_specs=pl.BlockSpec((TM_E * TOK_ROWS, LANES), lambda j, te, nx, nu: (j, 0)),
        scratch_shapes=[pltpu.VMEM(up_shape, _F32), pltpu.VMEM(up_shape, _F32), pltpu.VMEM(down_shape, _F32),
                        pltpu.VMEM(up_shape, _BF16), pltpu.VMEM(up_shape, _BF16), pltpu.VMEM(down_shape, _BF16),
                        pltpu.SemaphoreType.DMA((3,))],
    )
    return pl.pallas_call(
        _moe_body,
        out_shape=jax.ShapeDtypeStruct((n_tiles * TM_E * TOK_ROWS, LANES), _F32),
        grid_spec=grid_spec,
        compiler_params=_params("arbitrary"),
        name="moe",
    )(tile_expert, next_expert, n_used, xs, w_gate, w_up, w_down)


def _ple_body(slot_ref, h1_ref, rec_ref, p_ref, g_ref, wpg_ref, bpg_ref, wple_ref, fg_ref, ys_hbm, o_ref,
              ybuf, sem, *, tm):
    i = pl.program_id(0)
    cur = lax.rem(i, 2)

    def start_gather(tile, buf):
        for r in range(tm):
            for k in range(2):
                row0 = pl.multiple_of(slot_ref[(tile * tm + r) * 2 + k] * TOK_ROWS, TOK_ROWS)
                pltpu.make_async_copy(ys_hbm.at[pl.ds(row0, TOK_ROWS), :],
                                      ybuf.at[buf, k, pl.ds(r * TOK_PITCH, TOK_ROWS), :], sem.at[buf]).start()

    def wait_gather(buf):
        rows = pl.ds(0, tm * TOK_ROWS)
        for k in range(2):
            pltpu.make_async_copy(ys_hbm.at[rows, :], ybuf.at[buf, k, rows, :], sem.at[buf]).wait()

    def finish():
        rec = rec_ref[...]
        y1, y2 = (_load_token_major(lambda rows, k=k: ybuf[cur, k, rows, :], tm, TOK_PITCH) for k in range(2))
        h2 = h1_ref[...] + rec[:, R_W1:R_W1 + 1] * y1 + rec[:, R_W2:R_W2 + 1] * y2
        hn = _rms(h2, g_ref[...]).astype(_BF16)
        gate = jax.nn.sigmoid(_dot(hn, wpg_ref[...]) + bpg_ref[...])
        pe = _dot(p_ref[...].astype(_BF16), wple_ref[...])
        o_ref[...] = _rms(h2 + gate * pe, fg_ref[...])

    @pl.when(i == 0)
    def _():
        start_gather(0, 0)

    wait_gather(cur)

    @pl.when(i + 1 < pl.num_programs(0))
    def _():
        finish()
        start_gather(i + 1, 1 - cur)

    @pl.when(i + 1 == pl.num_programs(0))
    def _():
        finish()


def _ple(h1, ys, slots, rec, p, g, wpg, bpg, wple, fg, tm, row0):
    n = p.shape[0]
    off = row0 // tm
    tok = lambda width: pl.BlockSpec((tm, width), lambda i, s: (i + off, 0))
    const = lambda shape: pl.BlockSpec(shape, lambda i, s: (0,) * len(shape), pipeline_mode=pl.Buffered(1))
    grid_spec = pltpu.PrefetchScalarGridSpec(
        num_scalar_prefetch=1,
        grid=(n // tm,),
        in_specs=[tok(D_MODEL), tok(LANES), pl.BlockSpec((tm, D_PLE), lambda i, s: (i, 0)),
                  const((1, D_MODEL)), const((D_MODEL, D_MODEL)), const((1, D_MODEL)),
                  const((D_PLE, D_MODEL)), const((1, D_MODEL)), pl.BlockSpec(memory_space=pl.ANY)],
        out_specs=pl.BlockSpec((tm, D_MODEL), lambda i, s: (i, 0)),
        scratch_shapes=[pltpu.VMEM((2, 2, tm * TOK_PITCH, LANES), _F32), pltpu.SemaphoreType.DMA((2,))],
    )
    return pl.pallas_call(
        functools.partial(_ple_body, tm=tm),
        out_shape=jax.ShapeDtypeStruct((n, D_MODEL), _F32),
        grid_spec=grid_spec,
        compiler_params=_params("arbitrary"),
        name="ple",
    )(slots, h1, rec, p, g, wpg, bpg, wple, fg, ys)


def _schedule(rec, counts_row, n_all, n_tiles):
    ids = jnp.arange(N_EXPERTS, dtype=jnp.int32)
    expert = rec[:n_all, R_E1:R_E2 + 1].astype(jnp.int32)
    rank = rec[:n_all, R_RANK1:R_RANK2 + 1].astype(jnp.int32)
    counts = counts_row[0, :N_EXPERTS].astype(jnp.int32)
    tiles_per = (counts + TM_E - 1) // TM_E
    tiles_end = jnp.cumsum(tiles_per)
    n_used = tiles_end[-1]
    first_row = (tiles_end - tiles_per) * TM_E
    slot = jnp.sum(jnp.where(expert[..., None] == ids, first_row, 0), axis=-1) + rank
    pad_lo = jnp.concatenate([first_row + counts, (n_used * TM_E).reshape(1)]).astype(jnp.int32)
    pad_hi = jnp.concatenate([tiles_end * TM_E, jnp.full((1,), n_tiles * TM_E)]).astype(jnp.int32)
    tile_ids = jnp.arange(n_tiles, dtype=jnp.int32)
    te = jnp.minimum(jnp.sum((tiles_end[None, :] <= tile_ids[:, None]).astype(jnp.int32), axis=1), N_EXPERTS - 1)
    last_expert = jnp.sum(jnp.where(tile_ids == n_used - 1, te, 0))
    te = jnp.where(tile_ids < n_used, te, last_expert).astype(jnp.int32)
    later = (ids[None, :] > te[:, None]) & (counts[None, :] > 0)
    nxt = jnp.min(jnp.where(later, ids[None, :], N_EXPERTS), axis=1)
    nxt = jnp.where(nxt < N_EXPERTS, nxt, -1).astype(jnp.int32)
    return te, nxt, n_used.astype(jnp.int32).reshape(1), slot.reshape(-1), pad_lo, pad_hi


def kernel(x_prompt, x_sample, state_conf_conv, state_lru_conv, state_lru_h, p_prompt, p_sample, norm1_g, w_in, b_in, conf_dw_w, conf_dw_b, conf_ln_g, conf_ln_b, lru_conv_w, lru_conv_b, lru_wa, lru_ba, lru_wx, lru_bx, lru_lambda, w_out, norm2_g, w_grp, b_grp, w_rt, b_rt, w_gate, w_up, w_down, ple_norm_g, w_ple, w_pg, b_pg, final_g):
    batch, seq, _ = x_prompt.shape
    nb = x_sample.shape[0]
    n_p = batch * seq
    n_all = n_p + nb
    layer = 0

    row = lambda v: v.reshape(1, -1)
    w_in_b = w_in[layer].astype(_BF16)
    wc_b = w_out[layer, :D_CONF].astype(_BF16)
    wr_b = w_out[layer, D_CONF:].astype(_BF16)
    w_router = jnp.concatenate(
        [w_grp[layer], w_rt[layer], jnp.zeros((D_MODEL, LANES - N_GROUPS - N_EXPERTS), _F32)], axis=1).astype(_BF16)
    b_router = jnp.concatenate([b_grp[layer], b_rt[layer], jnp.zeros((LANES - N_GROUPS - N_EXPERTS,), _F32)])
    w_gates = jnp.concatenate([lru_wa[layer], lru_wx[layer]], axis=-1).astype(_BF16)
    b_gates = jnp.concatenate([lru_ba[layer], lru_bx[layer]], axis=-1)
    w_pg_b = w_pg[layer].astype(_BF16)
    w_ple_b = w_ple[layer].astype(_BF16)
    conf_args = (conf_dw_w[layer], row(conf_dw_b[layer]), row(conf_ln_g[layer]), row(conf_ln_b[layer]))
    lru_args = (lru_conv_w[layer], row(lru_conv_b[layer]), w_gates, b_gates, row(lru_lambda[layer]))
    out_args = (wc_b, wr_b, row(norm2_g[layer]), w_router, row(b_router))
    ple_args = (row(ple_norm_g[layer]), w_pg_b, row(b_pg[layer]), w_ple_b, row(final_g))

    xp = x_prompt.reshape(n_p, D_MODEL)
    u_p, lx_p, lg_p = _inproj(xp, row(norm1_g[layer]), w_in_b, row(b_in[layer]), TM)
    c_p = _conf_prompt(u_p, *conf_args, batch, seq)
    r_p, hl_p = _lru_prompt(lx_p, lg_p, *lru_args, batch, seq)

    xs = x_sample.reshape(nb, D_MODEL)
    u_s, lx_s, lg_s = _inproj(xs, row(norm1_g[layer]), w_in_b, row(b_in[layer]), nb)
    c_s, conf_buf_new = _conf_step(jnp.swapaxes(state_conf_conv, 1, 2), layer, u_s, *conf_args)
    r_s, hn_s, lru_buf_new = _lru_step(state_lru_conv, layer, lx_s, state_lru_h[layer], lg_s, *lru_args)
    h1, xn, rec, counts = _outproj(c_p, r_p, xp, c_s, r_s, xs, *out_args, TM)

    n_tiles = (2 * n_all + TM_E - 1) // TM_E + N_EXPERTS
    tile_expert, next_expert, n_used, slots, pad_lo, pad_hi = _schedule(rec, counts, n_all, n_tiles)
    xs = _dispatch(slots, pad_lo, pad_hi, xn, n_all, n_tiles * TM_E, TM)
    ys = _moe(tile_expert, next_expert, n_used, xs, w_gate[layer], w_up[layer], w_down[layer])

    y_p = _ple(h1, ys, slots[:2 * n_p], rec, p_prompt[layer].reshape(n_p, D_PLE), *ple_args, TM_PLE, 0)
    y_s = _ple(h1, ys, slots[2 * n_p:], rec, p_sample[layer].reshape(nb, D_PLE), *ple_args, nb, n_p)

    u_p3 = u_p.reshape(batch, seq, D_CONF)
    lx_p3 = lx_p.reshape(batch, seq, D_LRU)
    return (
        y_p.reshape(batch, seq, D_MODEL),
        y_s.reshape(nb, 1, D_MODEL),
        u_p3[None, :, seq - (CONF_K - 1):],
        jnp.swapaxes(conf_buf_new, 1, 2),
        lx_p3[None, :, seq - (LRU_K - 1):],
        lru_buf_new,
        hl_p.reshape(1, batch, D_LRU),
        hn_s[None],
    )
```

```python
import functools

import jax
import jax.numpy as jnp
from jax import lax
from jax.experimental import pallas as pl
from jax.experimental.pallas import tpu as pltpu

D_MODEL = 2048
D_CONF = 1024
D_LRU = 1024
N_HEADS = 8
HEAD = 128
CONF_K = 31
LRU_K = 4
LRU_C = 8.0
N_GROUPS = 4
EPG = 8
N_EXPERTS = 32
D_EXPERT = 512
D_PLE = 256
EPS = 1e-6

LANES = 128
SUBLANES = 8
VMEM_LIMIT = 56 * 1024 * 1024

TM = 512
TM_E = 256
CONF_TC = 128
LRU_TC = 512
LRU_PITCH = LRU_TC + SUBLANES
TM_PLE = 256
TOK_ROWS = D_MODEL // LANES
TOK_PITCH = TOK_ROWS + SUBLANES

_BF16 = jnp.bfloat16
_F32 = jnp.float32


def _params(*sem):
    return pltpu.CompilerParams(dimension_semantics=sem, vmem_limit_bytes=VMEM_LIMIT)


def _const_spec(shape):
    nd = len(shape)
    return pl.BlockSpec(shape, lambda *_: (0,) * nd, pipeline_mode=pl.Buffered(1))


def _store_token_major(ref, val, pitch=TOK_ROWS):
    for tb in range(val.shape[0] // SUBLANES):
        for s in range(TOK_ROWS):
            ref[pl.ds(tb * SUBLANES * pitch + s, SUBLANES, stride=pitch), :] = (
                val[tb * SUBLANES:(tb + 1) * SUBLANES, s * LANES:(s + 1) * LANES])


def _load_token_major(load, m, pitch):
    slabs = [jnp.concatenate([load(pl.ds(tb * SUBLANES * pitch + s, SUBLANES, stride=pitch))
                              for tb in range(m // SUBLANES)], axis=0) for s in range(TOK_ROWS)]
    return jnp.concatenate(slabs, axis=1)


def _rms(x, g):
    return x * lax.rsqrt(jnp.mean(x * x, axis=-1, keepdims=True) + EPS) * g


def _dot(a, b):
    return jnp.dot(a, b, preferred_element_type=_F32)


def _inproj_body(x_ref, g_ref, w_ref, b_ref, u_ref, lx_ref, lg_ref):
    xn = _rms(x_ref[...], g_ref[...]).astype(_BF16)

    def proj(k):
        sl = slice(k * D_CONF, (k + 1) * D_CONF)
        return _dot(xn, w_ref[:, sl]) + b_ref[:, sl]

    u_ref[...] = proj(0) * jax.nn.sigmoid(proj(1))
    lx_ref[...] = proj(2)
    lg_ref[...] = jax.nn.gelu(proj(3))


def _inproj(x, g, w, b, tm):
    n = x.shape[0]
    out = jax.ShapeDtypeStruct((n, D_CONF), _F32)
    tok = lambda width: pl.BlockSpec((tm, width), lambda i: (i, 0))
    return pl.pallas_call(
        _inproj_body,
        out_shape=(out, out, out),
        grid=(n // tm,),
        in_specs=[tok(D_MODEL), _const_spec((1, D_MODEL)), _const_spec((D_MODEL, 4 * D_CONF)),
                  _const_spec((1, 4 * D_CONF))],
        out_specs=(tok(D_CONF), tok(D_CONF), tok(D_CONF)),
        compiler_params=_params("arbitrary"),
        name="inproj",
    )(x, g, w, b)


def _layernorm_silu(conv, g, b):
    mu = jnp.mean(conv, axis=-1, keepdims=True)
    cen = conv - mu
    var = jnp.mean(cen * cen, axis=-1, keepdims=True)
    y = cen * lax.rsqrt(var + EPS) * g + b
    return y * jax.nn.sigmoid(y)


def _conf_body(u_ref, w_ref, b_ref, g_ref, lb_ref, c_ref, s_ref, *, seq):
    head = 4 * SUBLANES
    s_ref[0:head, :] = jnp.zeros((head, D_CONF), _F32)
    s_ref[head:head + seq, :] = u_ref[...]
    s_ref[head + seq:, :] = jnp.zeros((SUBLANES, D_CONF), _F32)
    tc = CONF_TC

    def chunk(ci, carry):
        t0 = pl.multiple_of(ci * tc, tc)
        win = s_ref[pl.ds(t0, tc + 5 * SUBLANES), :]
        out = None
        for r in range(SUBLANES):
            part = None
            for q in range(5):
                m = SUBLANES * q + r
                if 2 <= m <= CONF_K + 1:
                    term = win[SUBLANES * q:SUBLANES * q + tc + SUBLANES, :] * w_ref[m - 2:m - 1, :]
                    part = term if part is None else part + term
            shifted = part[r:r + tc, :]
            out = shifted if out is None else out + shifted
        c = _layernorm_silu(out + b_ref[...], g_ref[...], lb_ref[...])
        c_ref[pl.ds(t0, tc), :] = c.astype(_BF16)
        return carry

    lax.fori_loop(0, seq // tc, chunk, 0)


def _conf_prompt(u, w, b, g, lb, batch, seq):
    return pl.pallas_call(
        functools.partial(_conf_body, seq=seq),
        out_shape=jax.ShapeDtypeStruct((batch * seq, D_CONF), _BF16),
        grid=(batch,),
        in_specs=[pl.BlockSpec((seq, D_CONF), lambda i: (i, 0)), _const_spec((CONF_K, D_CONF)),
                  _const_spec((1, D_CONF)), _const_spec((1, D_CONF)), _const_spec((1, D_CONF))],
        out_specs=pl.BlockSpec((seq, D_CONF), lambda i: (i, 0)),
        scratch_shapes=[pltpu.VMEM((seq + 5 * SUBLANES, D_CONF), _F32)],
        compiler_params=_params("arbitrary"),
        name="conf_prompt",
    )(u, w, b, g, lb)


def _conf_step_body(buf_ref, u_ref, w_ref, b_ref, g_ref, lb_ref, c_ref, nbuf_ref):
    u = u_ref[...]
    conv = u * w_ref[CONF_K - 1:CONF_K, :]
    for k in range(CONF_K - 1):
        conv = conv + buf_ref[k] * w_ref[k:k + 1, :]
    c_ref[...] = _layernorm_silu(conv + b_ref[...], g_ref[...], lb_ref[...]).astype(_BF16)
    for k in range(CONF_K - 2):
        nbuf_ref[k] = buf_ref[k + 1]
    nbuf_ref[CONF_K - 2] = u


def _conf_step(buf, layer, u, w, b, g, lb):
    depth, _, nb, _ = buf.shape
    assert depth == 1
    bb = 16
    state = pl.BlockSpec((None, CONF_K - 1, bb, D_CONF), lambda i: (layer, 0, i, 0))
    return pl.pallas_call(
        _conf_step_body,
        out_shape=(jax.ShapeDtypeStruct((nb, D_CONF), _BF16), jax.ShapeDtypeStruct(buf.shape, buf.dtype)),
        grid=(nb // bb,),
        in_specs=[state, pl.BlockSpec((bb, D_CONF), lambda i: (i, 0)), _const_spec((CONF_K, D_CONF)),
                  _const_spec((1, D_CONF)), _const_spec((1, D_CONF)), _const_spec((1, D_CONF))],
        out_specs=(pl.BlockSpec((bb, D_CONF), lambda i: (i, 0)), state),
        compiler_params=_params("arbitrary"),
        name="conf_step",
    )(buf, u, w, b, g, lb)


def _lru_gates(lx, wg_ref, bg_ref, lam_ref, head):
    sl = slice(head * HEAD, (head + 1) * HEAD)
    lxh = lx[:, sl]
    z = _dot(lxh.astype(_BF16), wg_ref[head]) + bg_ref[head:head + 1, :]
    r = jax.nn.sigmoid(z[:, :HEAD])
    i = jax.nn.sigmoid(z[:, HEAD:])
    log_a = -LRU_C * r * jax.nn.softplus(-lam_ref[:, sl])
    a = jnp.exp(log_a)
    return a, jnp.sqrt(-jnp.tanh(log_a) * (a * a + 1.0)) * i * lxh


def _lru_body(lx_ref, lg_ref, cw_ref, cb_ref, wg_ref, bg_ref, lam_ref, r_ref, hl_ref,
              halo_ref, h_ref, a_ref, u_ref, hs_ref):
    j = pl.program_id(1)
    tc = LRU_TC

    @pl.when(j == 0)
    def _():
        halo_ref[...] = jnp.zeros_like(halo_ref)
        h_ref[...] = jnp.zeros_like(h_ref)

    cur = lx_ref[...]
    ext = jnp.concatenate([halo_ref[...], cur], axis=0)
    lx = cb_ref[...] + cw_ref[LRU_K - 1:LRU_K, :] * cur
    for back in range(1, LRU_K):
        lx = lx + cw_ref[LRU_K - 1 - back:LRU_K - back, :] * ext[SUBLANES - back:SUBLANES - back + tc, :]
    halo_ref[...] = cur[tc - SUBLANES:, :]

    for head in range(N_HEADS):
        a, u = _lru_gates(lx, wg_ref, bg_ref, lam_ref, head)
        a_ref[head * LRU_PITCH:head * LRU_PITCH + tc, :] = a
        u_ref[head * LRU_PITCH:head * LRU_PITCH + tc, :] = u

    def step(t, h):
        rows = pl.ds(t, N_HEADS, stride=LRU_PITCH)
        h = a_ref[rows, :] * h + u_ref[rows, :]
        hs_ref[rows, :] = h
        return h

    h_ref[...] = lax.fori_loop(0, tc, step, h_ref[...], unroll=8)

    for head in range(N_HEADS):
        sl = slice(head * HEAD, (head + 1) * HEAD)
        hs = hs_ref[head * LRU_PITCH:head * LRU_PITCH + tc, :]
        r_ref[:, sl] = (hs * lg_ref[:, sl]).astype(_BF16)

    @pl.when(j == pl.num_programs(1) - 1)
    def _():
        hl_ref[0] = h_ref[...]


def _lru_prompt(lx, lg, cw, cb, wg, bg, lam, batch, seq):
    nchunk = seq // LRU_TC
    tok = pl.BlockSpec((LRU_TC, D_LRU), lambda b, j: (b * nchunk + j, 0))
    pitch_rows = N_HEADS * LRU_PITCH
    return pl.pallas_call(
        _lru_body,
        out_shape=(jax.ShapeDtypeStruct((batch * seq, D_LRU), _BF16),
                   jax.ShapeDtypeStruct((batch, N_HEADS, HEAD), _F32)),
        grid=(batch, nchunk),
        in_specs=[tok, tok, _const_spec((LRU_K, D_LRU)), _const_spec((1, D_LRU)),
                  _const_spec((N_HEADS, HEAD, 2 * HEAD)), _const_spec((N_HEADS, 2 * HEAD)),
                  _const_spec((1, D_LRU))],
        out_specs=(tok, pl.BlockSpec((1, N_HEADS, HEAD), lambda b, j: (b, 0, 0))),
        scratch_shapes=[pltpu.VMEM((SUBLANES, D_LRU), _F32), pltpu.VMEM((N_HEADS, HEAD), _F32),
                        pltpu.VMEM((pitch_rows, HEAD), _F32), pltpu.VMEM((pitch_rows, HEAD), _F32),
                        pltpu.VMEM((pitch_rows, HEAD), _F32)],
        compiler_params=_params("arbitrary", "arbitrary"),
        name="lru_prompt",
    )(lx, lg, cw, cb, wg, bg, lam)


def _lru_step_body(buf_ref, lx_ref, h0_ref, lg_ref, cw_ref, cb_ref, wg_ref, bg_ref, lam_ref,
                   r_ref, hn_ref, nbuf_ref):
    cur = lx_ref[...]
    lx = cb_ref[...] + cw_ref[LRU_K - 1:LRU_K, :] * cur
    for k in range(LRU_K - 1):
        lx = lx + cw_ref[k:k + 1, :] * buf_ref[:, k, :]
    for k in range(LRU_K - 2):
        nbuf_ref[:, k, :] = buf_ref[:, k + 1, :]
    nbuf_ref[:, LRU_K - 2, :] = cur
    for head in range(N_HEADS):
        sl = slice(head * HEAD, (head + 1) * HEAD)
        a, u = _lru_gates(lx, wg_ref, bg_ref, lam_ref, head)
        h = a * h0_ref[:, sl] + u
        hn_ref[:, sl] = h
        r_ref[:, sl] = (h * lg_ref[:, sl]).astype(_BF16)


def _lru_step(buf, layer, lx, h0, lg, cw, cb, wg, bg, lam):
    depth, nb = buf.shape[:2]
    assert depth == 1
    full = _const_spec((nb, D_LRU))
    state = pl.BlockSpec((None, nb, LRU_K - 1, D_LRU), lambda i: (layer, 0, 0, 0))
    tok = pl.BlockSpec((nb, D_LRU), lambda i: (0, 0))
    return pl.pallas_call(
        _lru_step_body,
        out_shape=(jax.ShapeDtypeStruct((nb, D_LRU), _BF16), jax.ShapeDtypeStruct((nb, D_LRU), _F32),
                   jax.ShapeDtypeStruct(buf.shape, buf.dtype)),
        grid=(1,),
        in_specs=[state, full, full, full, _const_spec((LRU_K, D_LRU)), _const_spec((1, D_LRU)),
                  _const_spec((N_HEADS, HEAD, 2 * HEAD)), _const_spec((N_HEADS, 2 * HEAD)),
                  _const_spec((1, D_LRU))],
        out_specs=(tok, tok, state),
        compiler_params=_params("arbitrary"),
        name="lru_step",
    )(buf, lx, h0, lg, cw, cb, wg, bg, lam)


R_E1, R_E2, R_W1, R_W2, R_RANK1, R_RANK2 = range(6)


def _route_tile(logits, seen):
    m = logits.shape[0]
    lane = lax.broadcasted_iota(jnp.int32, (m, LANES), 1).astype(_F32)
    neg = jnp.float32(-jnp.inf)

    def first_max(vals):
        vmax = jnp.max(vals, axis=1, keepdims=True)
        return vmax, jnp.min(jnp.where(vals == vmax, lane, float(LANES)), axis=1, keepdims=True)

    in_groups = lane < N_GROUPS
    g_max, g_idx = first_max(jnp.where(in_groups, logits, neg))
    g_den = jnp.sum(jnp.where(in_groups, jnp.exp(logits - g_max), 0.0), axis=1, keepdims=True)
    lo = N_GROUPS + EPG * g_idx
    e_logits = jnp.where((lane >= lo) & (lane < lo + EPG), logits, neg)
    v1, i1 = first_max(e_logits)
    v2, i2 = first_max(jnp.where(lane == i1, neg, e_logits))
    t = jnp.exp(v2 - v1)
    w1 = 1.0 / (g_den * (1.0 + t))
    w2 = w1 * t
    e1 = i1 - N_GROUPS
    e2 = i2 - N_GROUPS

    is1 = lane == e1
    is2 = lane == e2
    onehot = jnp.where(is1 | is2, 1.0, 0.0)
    earlier = (lax.broadcasted_iota(jnp.int32, (m, m), 1) < lax.broadcasted_iota(jnp.int32, (m, m), 0))
    before = seen + _dot(jnp.where(earlier, 1.0, 0.0).astype(_BF16), onehot.astype(_BF16))
    rank1 = jnp.sum(jnp.where(is1, before, 0.0), axis=1, keepdims=True)
    rank2 = jnp.sum(jnp.where(is2, before, 0.0), axis=1, keepdims=True)

    rec = jnp.zeros((m, LANES), _F32)
    for pos, val in ((R_E1, e1), (R_E2, e2), (R_W1, w1), (R_W2, w2), (R_RANK1, rank1), (R_RANK2, rank2)):
        rec = jnp.where(lane == pos, val, rec)
    return rec, seen + jnp.sum(onehot, axis=0, keepdims=True)


def _outproj_body(cp_ref, rp_ref, xp_ref, cs_ref, rs_ref, xs_ref, wc_ref, wr_ref, g_ref, wrt_ref, brt_ref,
                  h1_ref, xn_ref, rec_ref, cnt_ref, seen_ref, *, n_ptiles, nb):
    i = pl.program_id(0)

    @pl.when(i == 0)
    def _():
        seen_ref[...] = jnp.zeros_like(seen_ref)

    def mix(c, r, x):
        h1 = x + _dot(c, wc_ref[...]) + _dot(r, wr_ref[...])
        xn = _rms(h1, g_ref[...])
        logits = _dot(xn.astype(_BF16), wrt_ref[...]) + brt_ref[...]
        rec, seen = _route_tile(logits, seen_ref[...])
        seen_ref[...] = seen
        return h1, xn, rec

    @pl.when(i < n_ptiles)
    def _():
        h1_ref[...], xn_ref[...], rec_ref[...] = mix(cp_ref[...], rp_ref[...], xp_ref[...])

    @pl.when(i == n_ptiles)
    def _():
        for ref, val in zip((h1_ref, xn_ref, rec_ref), mix(cs_ref[...], rs_ref[...], xs_ref[...])):
            ref[:nb, :] = val
            ref[nb:, :] = jnp.zeros((ref.shape[0] - nb, ref.shape[1]), _F32)

    cnt_ref[...] = jnp.broadcast_to(seen_ref[...], cnt_ref.shape)


def _outproj(cp, rp, xp, cs, rs, xs, wc, wr, g, wrt, brt, tm):
    n_ptiles = xp.shape[0] // tm
    nb = xs.shape[0]
    rows = (n_ptiles + 1) * tm
    last = n_ptiles - 1
    tok = lambda width: pl.BlockSpec((tm, width), lambda i: (jnp.minimum(i, last), 0))
    out_tok = lambda width: pl.BlockSpec((tm, width), lambda i: (i, 0))
    return pl.pallas_call(
        functools.partial(_outproj_body, n_ptiles=n_ptiles, nb=nb),
        out_shape=(jax.ShapeDtypeStruct((rows, D_MODEL), _F32), jax.ShapeDtypeStruct((rows, D_MODEL), _F32),
                   jax.ShapeDtypeStruct((rows, LANES), _F32), jax.ShapeDtypeStruct((SUBLANES, LANES), _F32)),
        grid=(n_ptiles + 1,),
        in_specs=[tok(D_CONF), tok(D_LRU), tok(D_MODEL), _const_spec((nb, D_CONF)), _const_spec((nb, D_LRU)),
                  _const_spec((nb, D_MODEL)), _const_spec((D_CONF, D_MODEL)), _const_spec((D_LRU, D_MODEL)),
                  _const_spec((1, D_MODEL)), _const_spec((D_MODEL, LANES)), _const_spec((1, LANES))],
        out_specs=(out_tok(D_MODEL), out_tok(D_MODEL), out_tok(LANES),
                   pl.BlockSpec((SUBLANES, LANES), lambda i: (0, 0))),
        scratch_shapes=[pltpu.VMEM((1, LANES), _F32)],
        compiler_params=_params("arbitrary"),
        name="outproj",
    )(cp, rp, xp, cs, rs, xs, wc, wr, g, wrt, brt)


def _dispatch_body(slot_ref, lo_ref, hi_ref, x_hbm, xs_hbm, xbuf, zeros, xsem, sem, zsem, *, tm, n_full, tail):
    i = pl.program_id(0)

    def zero_copies(n, wait):
        lo, hi = lo_ref[n], hi_ref[n]
        head_end = jnp.minimum(((lo + SUBLANES - 1) // SUBLANES) * SUBLANES, hi)

        def go(copy):
            copy.wait() if wait else copy.start()

        def single(s, carry):
            go(pltpu.make_async_copy(zeros.at[pl.ds(0, 1), :], xs_hbm.at[pl.ds(s, 1), :], zsem))
            return carry
        lax.fori_loop(lo, head_end, single, 0)

        size = hi - head_end
        n_bufs = size // TM_E

        def whole(c, carry):
            start = pl.multiple_of(head_end + c * TM_E, SUBLANES)
            go(pltpu.make_async_copy(zeros, xs_hbm.at[pl.ds(start, TM_E), :], zsem))
            return carry
        lax.fori_loop(0, n_bufs, whole, 0)
        start = pl.multiple_of(head_end + n_bufs * TM_E, SUBLANES)
        rest = pl.multiple_of(size - n_bufs * TM_E, SUBLANES)

        @pl.when(rest > 0)
        def _():
            go(pltpu.make_async_copy(zeros.at[pl.ds(0, rest), :], xs_hbm.at[pl.ds(start, rest), :], zsem))

    cur = lax.rem(i, 2)
    x_ref = xbuf.at[lax.rem(i, 3)]

    def fetch(tile):
        buf = lax.rem(tile, 3)
        return pltpu.make_async_copy(x_hbm.at[pl.ds(pl.multiple_of(tile * tm, tm), tm), :], xbuf.at[buf], xsem.at[buf])

    def start_scatter(rows):
        for r in range(rows):
            for k in range(2):
                s = slot_ref[(i * tm + r) * 2 + k]
                pltpu.make_async_copy(x_ref.at[pl.ds(r, 1), :], xs_hbm.at[pl.ds(s, 1), :],
                                      sem.at[cur]).start(priority=k)

    def wait_scatter(rows, parity):
        for k in range(2):
            pltpu.make_async_copy(x_ref.at[pl.ds(0, rows), :], xs_hbm.at[pl.ds(0, rows), :], sem.at[parity]).wait()

    @pl.when(i == 0)
    def _():
        fetch(0).start()
        zeros[...] = jnp.zeros_like(zeros)
        for wait in (False, True):
            lax.fori_loop(0, N_EXPERTS + 1, lambda n, c: (zero_copies(n, wait), c)[1], 0)

    @pl.when(i + 1 < pl.num_programs(0))
    def _():
        fetch(i + 1).start()

    fetch(i).wait()

    @pl.when(i < n_full)
    def _():
        start_scatter(tm)

    @pl.when(i == n_full)
    def _():
        start_scatter(tail)

    @pl.when(i > 0)
    def _():
        wait_scatter(tm, 1 - cur)

    @pl.when(i == pl.num_programs(0) - 1)
    def _():
        wait_scatter(tail if tail else tm, cur)


def _dispatch(slots, pad_lo, pad_hi, xn, n_all, n_rows, tm):
    n_full, tail = divmod(n_all, tm)
    grid_spec = pltpu.PrefetchScalarGridSpec(
        num_scalar_prefetch=3,
        grid=(n_full + (1 if tail else 0),),
        in_specs=[pl.BlockSpec(memory_space=pl.ANY)],
        out_specs=pl.BlockSpec(memory_space=pl.ANY),
        scratch_shapes=[pltpu.VMEM((3, tm, D_MODEL), _F32), pltpu.VMEM((TM_E, D_MODEL), _F32),
                        pltpu.SemaphoreType.DMA((3,)), pltpu.SemaphoreType.DMA((2,)), pltpu.SemaphoreType.DMA(())],
    )
    return pl.pallas_call(
        functools.partial(_dispatch_body, tm=tm, n_full=n_full, tail=tail),
        out_shape=jax.ShapeDtypeStruct((n_rows, D_MODEL), _F32),
        grid_spec=grid_spec,
        compiler_params=_params("arbitrary"),
        name="dispatch",
    )(slots, pad_lo, pad_hi, xn)


def _moe_body(te_ref, nx_ref, nu_ref, x_ref, wg_hbm, wu_hbm, wd_hbm, y_ref,
              wg_f, wu_f, wd_f, wg_b, wu_b, wd_b, wsem):
    j = pl.program_id(0)
    n_used = nu_ref[0]

    def weight_copies(e):
        return (pltpu.make_async_copy(wg_hbm.at[e], wg_f, wsem.at[0]),
                pltpu.make_async_copy(wu_hbm.at[e], wu_f, wsem.at[1]),
                pltpu.make_async_copy(wd_hbm.at[e], wd_f, wsem.at[2]))

    @pl.when(j == 0)
    def _():
        for copy in weight_copies(te_ref[0]):
            copy.start()

    @pl.when(j < n_used)
    def _():
        @pl.when((j == 0) | (te_ref[j] != te_ref[jnp.maximum(j - 1, 0)]))
        def _():
            for copy in weight_copies(te_ref[j]):
                copy.wait()
            wg_b[...] = wg_f[...].astype(_BF16)
            wu_b[...] = wu_f[...].astype(_BF16)
            wd_b[...] = wd_f[...].astype(_BF16)

            @pl.when(nx_ref[j] >= 0)
            def _():
                for copy in weight_copies(nx_ref[j]):
                    copy.start(priority=1)

        x = x_ref[...].astype(_BF16)
        g = _dot(x, wg_b[...])
        u = _dot(x, wu_b[...])
        hid = g * jax.nn.sigmoid(g) * u
        _store_token_major(y_ref, _dot(hid.astype(_BF16), wd_b[...]))

    @pl.when(j >= n_used)
    def _():
        y_ref[...] = jnp.zeros_like(y_ref)


def _moe(tile_expert, next_expert, n_used, xs, w_gate, w_up, w_down):
    n_tiles = tile_expert.shape[0]
    any_space = pl.BlockSpec(memory_space=pl.ANY)
    up_shape, down_shape = (D_MODEL, D_EXPERT), (D_EXPERT, D_MODEL)
    grid_spec = pltpu.PrefetchScalarGridSpec(
        num_scalar_prefetch=3,
        grid=(n_tiles,),
        in_specs=[pl.BlockSpec((TM_E, D_MODEL), lambda j, te, nx, nu: (jnp.minimum(j, nu[0] - 1), 0)),
                  any_space, any_space, any_space],
        out_specs=pl.BlockSpec((TM_E * TOK_ROWS, LANES), lambda j, te, nx, nu: (j, 0)),
        scratch_shapes=[pltpu.VMEM(up_shape, _F32), pltpu.VMEM(up_shape, _F32), pltpu.VMEM(down_shape, _F32),
                        pltpu.VMEM(up_shape, _BF16), pltpu.VMEM(up_shape, _BF16), pltpu.VMEM(down_shape, _BF16),
                        pltpu.SemaphoreType.DMA((3,))],
    )
    return pl.pallas_call(
        _moe_body,
        out_shape=jax.ShapeDtypeStruct((n_tiles * TM_E * TOK_ROWS, LANES), _F32),
        grid_spec=grid_spec,
        compiler_params=_params("arbitrary"),
        name="moe",
    )(tile_expert, next_expert, n_used, xs, w_gate, w_up, w_down)


def _ple_body(slot_ref, h1_ref, rec_ref, p_ref, g_ref, wpg_ref, bpg_ref, wple_ref, fg_ref, ys_hbm, o_ref,
              ybuf, sem, *, tm):
    i = pl.program_id(0)
    cur = lax.rem(i, 2)

    def start_gather(tile, buf):
        for r in range(tm):
            for k in range(2):
                row0 = pl.multiple_of(slot_ref[(tile * tm + r) * 2 + k] * TOK_ROWS, TOK_ROWS)
                pltpu.make_async_copy(ys_hbm.at[pl.ds(row0, TOK_ROWS), :],
                                      ybuf.at[buf, k, pl.ds(r * TOK_PITCH, TOK_ROWS), :], sem.at[buf]).start()

    def wait_gather(buf):
        rows = pl.ds(0, tm * TOK_ROWS)
        for k in range(2):
            pltpu.make_async_copy(ys_hbm.at[rows, :], ybuf.at[buf, k, rows, :], sem.at[buf]).wait()

    def finish():
        rec = rec_ref[...]
        y1, y2 = (_load_token_major(lambda rows, k=k: ybuf[cur, k, rows, :], tm, TOK_PITCH) for k in range(2))
        h2 = h1_ref[...] + rec[:, R_W1:R_W1 + 1] * y1 + rec[:, R_W2:R_W2 + 1] * y2
        hn = _rms(h2, g_ref[...]).astype(_BF16)
        gate = jax.nn.sigmoid(_dot(hn, wpg_ref[...]) + bpg_ref[...])
        pe = _dot(p_ref[...].astype(_BF16), wple_ref[...])
        o_ref[...] = _rms(h2 + gate * pe, fg_ref[...])

    @pl.when(i == 0)
    def _():
        start_gather(0, 0)

    wait_gather(cur)

    @pl.when(i + 1 < pl.num_programs(0))
    def _():
        finish()
        start_gather(i + 1, 1 - cur)

    @pl.when(i + 1 == pl.num_programs(0))
    def _():
        finish()


def _ple(h1, ys, slots, rec, p, g, wpg, bpg, wple, fg, tm, row0):
    n = p.shape[0]
    off = row0 // tm
    tok = lambda width: pl.BlockSpec((tm, width), lambda i, s: (i + off, 0))
    const = lambda shape: pl.BlockSpec(shape, lambda i, s: (0,) * len(shape), pipeline_mode=pl.Buffered(1))
    grid_spec = pltpu.PrefetchScalarGridSpec(
        num_scalar_prefetch=1,
        grid=(n // tm,),
        in_specs=[tok(D_MODEL), tok(LANES), pl.BlockSpec((tm, D_PLE), lambda i, s: (i, 0)),
                  const((1, D_MODEL)), const((D_MODEL, D_MODEL)), const((1, D_MODEL)),
                  const((D_PLE, D_MODEL)), const((1, D_MODEL)), pl.BlockSpec(memory_space=pl.ANY)],
        out_specs=pl.BlockSpec((tm, D_MODEL), lambda i, s: (i, 0)),
        scratch_shapes=[pltpu.VMEM((2, 2, tm * TOK_PITCH, LANES), _F32), pltpu.SemaphoreType.DMA((2,))],
    )
    return pl.pallas_call(
        functools.partial(_ple_body, tm=tm),
        out_shape=jax.ShapeDtypeStruct((n, D_MODEL), _F32),
        grid_spec=grid_spec,
        compiler_params=_params("arbitrary"),
        name="ple",
    )(slots, h1, rec, p, g, wpg, bpg, wple, fg, ys)


def _schedule(rec, counts_row, n_all, n_tiles):
    ids = jnp.arange(N_EXPERTS, dtype=jnp.int32)
    expert = rec[:n_all, R_E1:R_E2 + 1].astype(jnp.int32)
    rank = rec[:n_all, R_RANK1:R_RANK2 + 1].astype(jnp.int32)
    counts = counts_row[0, :N_EXPERTS].astype(jnp.int32)
    tiles_per = (counts + TM_E - 1) // TM_E
    tiles_end = jnp.cumsum(tiles_per)
    n_used = tiles_end[-1]
    first_row = (tiles_end - tiles_per) * TM_E
    slot = jnp.sum(jnp.where(expert[..., None] == ids, first_row, 0), axis=-1) + rank
    pad_lo = jnp.concatenate([first_row + counts, (n_used * TM_E).reshape(1)]).astype(jnp.int32)
    pad_hi = jnp.concatenate([tiles_end * TM_E, jnp.full((1,), n_tiles * TM_E)]).astype(jnp.int32)
    tile_ids = jnp.arange(n_tiles, dtype=jnp.int32)
    te = jnp.minimum(jnp.sum((tiles_end[None, :] <= tile_ids[:, None]).astype(jnp.int32), axis=1), N_EXPERTS - 1)
    last_expert = jnp.sum(jnp.where(tile_ids == n_used - 1, te, 0))
    te = jnp.where(tile_ids < n_used, te, last_expert).astype(jnp.int32)
    later = (ids[None, :] > te[:, None]) & (counts[None, :] > 0)
    nxt = jnp.min(jnp.where(later, ids[None, :], N_EXPERTS), axis=1)
    nxt = jnp.where(nxt < N_EXPERTS, nxt, -1).astype(jnp.int32)
    return te, nxt, n_used.astype(jnp.int32).reshape(1), slot.reshape(-1), pad_lo, pad_hi


def kernel(x_prompt, x_sample, state_conf_conv, state_lru_conv, state_lru_h, p_prompt, p_sample, norm1_g, w_in, b_in, conf_dw_w, conf_dw_b, conf_ln_g, conf_ln_b, lru_conv_w, lru_conv_b, lru_wa, lru_ba, lru_wx, lru_bx, lru_lambda, w_out, norm2_g, w_grp, b_grp, w_rt, b_rt, w_gate, w_up, w_down, ple_norm_g, w_ple, w_pg, b_pg, final_g):
    batch, seq, _ = x_prompt.shape
    nb = x_sample.shape[0]
    n_p = batch * seq
    n_all = n_p + nb
    layer = 0

    row = lambda v: v.reshape(1, -1)
    w_in_b = w_in[layer].astype(_BF16)
    wc_b = w_out[layer, :D_CONF].astype(_BF16)
    wr_b = w_out[layer, D_CONF:].astype(_BF16)
    w_router = jnp.concatenate(
        [w_grp[layer], w_rt[layer], jnp.zeros((D_MODEL, LANES - N_GROUPS - N_EXPERTS), _F32)], axis=1).astype(_BF16)
    b_router = jnp.concatenate([b_grp[layer], b_rt[layer], jnp.zeros((LANES - N_GROUPS - N_EXPERTS,), _F32)])
    w_gates = jnp.concatenate([lru_wa[layer], lru_wx[layer]], axis=-1).astype(_BF16)
    b_gates = jnp.concatenate([lru_ba[layer], lru_bx[layer]], axis=-1)
    w_pg_b = w_pg[layer].astype(_BF16)
    w_ple_b = w_ple[layer].astype(_BF16)
    conf_args = (conf_dw_w[layer], row(conf_dw_b[layer]), row(conf_ln_g[layer]), row(conf_ln_b[layer]))
    lru_args = (lru_conv_w[layer], row(lru_conv_b[layer]), w_gates, b_gates, row(lru_lambda[layer]))
    out_args = (wc_b, wr_b, row(norm2_g[layer]), w_router, row(b_router))
    ple_args = (row(ple_norm_g[layer]), w_pg_b, row(b_pg[layer]), w_ple_b, row(final_g))

    xp = x_prompt.reshape(n_p, D_MODEL)
    u_p, lx_p, lg_p = _inproj(xp, row(norm1_g[layer]), w_in_b, row(b_in[layer]), TM)
    c_p = _conf_prompt(u_p, *conf_args, batch, seq)
    r_p, hl_p = _lru_prompt(lx_p, lg_p, *lru_args, batch, seq)

    xs = x_sample.reshape(nb, D_MODEL)
    u_s, lx_s, lg_s = _inproj(xs, row(norm1_g[layer]), w_in_b, row(b_in[layer]), nb)
    c_s, conf_buf_new = _conf_step(jnp.swapaxes(state_conf_conv, 1, 2), layer, u_s, *conf_args)
    r_s, hn_s, lru_buf_new = _lru_step(state_lru_conv, layer, lx_s, state_lru_h[layer], lg_s, *lru_args)
    h1, xn, rec, counts = _outproj(c_p, r_p, xp, c_s, r_s, xs, *out_args, TM)

    n_tiles = (2 * n_all + TM_E - 1) // TM_E + N_EXPERTS
    tile_expert, next_expert, n_used, slots, pad_lo, pad_hi = _schedule(rec, counts, n_all, n_tiles)
    xs = _dispatch(slots, pad_lo, pad_hi, xn, n_all, n_tiles * TM_E, TM)
    ys = _moe(tile_expert, next_expert, n_used, xs, w_gate[layer], w_up[layer], w_down[layer])

    y_p = _ple(h1, ys, slots[:2 * n_p], rec, p_prompt[layer].reshape(n_p, D_PLE), *ple_args, TM_PLE, 0)
    y_s = _ple(h1, ys, slots[2 * n_p:], rec, p_sample[layer].reshape(nb, D_PLE), *ple_args, nb, n_p)

    u_p3 = u_p.reshape(batch, seq, D_CONF)
    lx_p3 = lx_p.reshape(batch, seq, D_LRU)
    return (
        y_p.reshape(batch, seq, D_MODEL),
        y_s.reshape(nb, 1, D_MODEL),
        u_p3[None, :, seq - (CONF_K - 1):],
        jnp.swapaxes(conf_buf_new, 1, 2),
        lx_p3[None, :, seq - (LRU_K - 1):],
        lru_buf_new,
        hl_p.reshape(1, batch, D_LRU),
        hn_s[None],
    )
```

```python
import functools

import jax
import jax.numpy as jnp
from jax import lax
from jax.experimental import pallas as pl
from jax.experimental.pallas import tpu as pltpu

D_MODEL = 2048
D_CONF = 1024
D_LRU = 1024
N_HEADS = 8
HEAD = 128
CONF_K = 31
LRU_K = 4
LRU_C = 8.0
N_GROUPS = 4
EPG = 8
N_EXPERTS = 32
D_EXPERT = 512
D_PLE = 256
EPS = 1e-6

LANES = 128
SUBLANES = 8
VMEM_LIMIT = 56 * 1024 * 1024

TM = 512
TM_E = 256
CONF_TC = 256
LRU_TC = 512
LRU_PITCH = LRU_TC + SUBLANES
TM_PLE = 256
TOK_ROWS = D_MODEL // LANES
TOK_PITCH = TOK_ROWS + SUBLANES

_BF16 = jnp.bfloat16
_F32 = jnp.float32


def _params(*sem):
    return pltpu.CompilerParams(dimension_semantics=sem, vmem_limit_bytes=VMEM_LIMIT)


def _const_spec(shape):
    nd = len(shape)
    return pl.BlockSpec(shape, lambda *_: (0,) * nd, pipeline_mode=pl.Buffered(1))


def _store_token_major(ref, val, pitch=TOK_ROWS):
    for tb in range(val.shape[0] // SUBLANES):
        for s in range(TOK_ROWS):
            ref[pl.ds(tb * SUBLANES * pitch + s, SUBLANES, stride=pitch), :] = (
                val[tb * SUBLANES:(tb + 1) * SUBLANES, s * LANES:(s + 1) * LANES])


def _load_token_major(load, m, pitch):
    slabs = [jnp.concatenate([load(pl.ds(tb * SUBLANES * pitch + s, SUBLANES, stride=pitch))
                              for tb in range(m // SUBLANES)], axis=0) for s in range(TOK_ROWS)]
    return jnp.concatenate(slabs, axis=1)


def _rms(x, g):
    return x * lax.rsqrt(jnp.mean(x * x, axis=-1, keepdims=True) + EPS) * g


def _dot(a, b):
    return jnp.dot(a, b, preferred_element_type=_F32)


W_CHUNK = 256


def _cast_weight_resident(w_hbm, w_b, stage, sem):
    n_chunks = w_b.shape[1] // W_CHUNK

    def copy(c):
        return pltpu.make_async_copy(w_hbm.at[:, pl.ds(c * W_CHUNK, W_CHUNK)], stage.at[c % 2], sem.at[c % 2])

    copy(0).start()
    for c in range(n_chunks):
        if c + 1 < n_chunks:
            copy(c + 1).start()
        copy(c).wait()
        w_b[:, c * W_CHUNK:(c + 1) * W_CHUNK] = stage[c % 2].astype(_BF16)


def _weight_scratch(rows, cols):
    return [pltpu.VMEM((rows, cols), _BF16), pltpu.VMEM((2, rows, W_CHUNK), _F32), pltpu.SemaphoreType.DMA((2,))]


def _inproj_body(xp_ref, xs_ref, g_ref, w_hbm, b_ref, up_ref, lxp_ref, lgp_ref, us_ref, lxs_ref, lgs_ref,
                 w_b, stage, wsem, *, n_ptiles):
    i = pl.program_id(0)

    @pl.when(i == 0)
    def _():
        _cast_weight_resident(w_hbm, w_b, stage, wsem)

    def project(x_ref, u_ref, lx_ref, lg_ref):
        xn = _rms(x_ref[...], g_ref[...]).astype(_BF16)

        def proj(k):
            sl = slice(k * D_CONF, (k + 1) * D_CONF)
            return _dot(xn, w_b[:, sl]) + b_ref[:, sl]

        u_ref[...] = proj(0) * jax.nn.sigmoid(proj(1))
        lx_ref[...] = proj(2)
        lg_ref[...] = jax.nn.gelu(proj(3))

    @pl.when(i < n_ptiles)
    def _():
        project(xp_ref, up_ref, lxp_ref, lgp_ref)

    @pl.when(i == n_ptiles)
    def _():
        project(xs_ref, us_ref, lxs_ref, lgs_ref)


def _inproj(xp, xs, g, w, b, tm):
    n_ptiles = xp.shape[0] // tm
    nb = xs.shape[0]
    last = n_ptiles - 1
    tok = lambda width: pl.BlockSpec((tm, width), lambda i: (jnp.minimum(i, last), 0))
    sample = pl.BlockSpec((nb, D_CONF), lambda i: (0, 0))
    p_out = jax.ShapeDtypeStruct((xp.shape[0], D_CONF), _F32)
    s_out = jax.ShapeDtypeStruct((nb, D_CONF), _F32)
    return pl.pallas_call(
        functools.partial(_inproj_body, n_ptiles=n_ptiles),
        out_shape=(p_out, p_out, p_out, s_out, s_out, s_out),
        grid=(n_ptiles + 1,),
        in_specs=[tok(D_MODEL), _const_spec((nb, D_MODEL)), _const_spec((1, D_MODEL)),
                  pl.BlockSpec(memory_space=pl.ANY), _const_spec((1, 4 * D_CONF))],
        out_specs=(tok(D_CONF), tok(D_CONF), tok(D_CONF), sample, sample, sample),
        scratch_shapes=_weight_scratch(D_MODEL, 4 * D_CONF),
        compiler_params=_params("arbitrary"),
        name="inproj",
    )(xp, xs, g, w, b)


def _layernorm_silu(conv, g, b):
    mu = jnp.mean(conv, axis=-1, keepdims=True)
    cen = conv - mu
    var = jnp.mean(cen * cen, axis=-1, keepdims=True)
    y = cen * lax.rsqrt(var + EPS) * g + b
    return y * jax.nn.sigmoid(y)


def _conf_body(u_ref, w_ref, b_ref, g_ref, lb_ref, c_ref, s_ref, *, seq):
    head = 4 * SUBLANES
    s_ref[0:head, :] = jnp.zeros((head, D_CONF), _F32)
    s_ref[head:head + seq, :] = u_ref[...]
    s_ref[head + seq:, :] = jnp.zeros((SUBLANES, D_CONF), _F32)
    tc = CONF_TC

    def chunk(ci, carry):
        t0 = pl.multiple_of(ci * tc, tc)
        win = s_ref[pl.ds(t0, tc + 5 * SUBLANES), :]
        out = None
        for r in range(SUBLANES):
            part = None
            for q in range(5):
                m = SUBLANES * q + r
                if 2 <= m <= CONF_K + 1:
                    term = win[SUBLANES * q:SUBLANES * q + tc + SUBLANES, :] * w_ref[m - 2:m - 1, :]
                    part = term if part is None else part + term
            shifted = part[r:r + tc, :]
            out = shifted if out is None else out + shifted
        c = _layernorm_silu(out + b_ref[...], g_ref[...], lb_ref[...])
        c_ref[pl.ds(t0, tc), :] = c.astype(_BF16)
        return carry

    lax.fori_loop(0, seq // tc, chunk, 0)


def _conf_prompt(u, w, b, g, lb, batch, seq):
    return pl.pallas_call(
        functools.partial(_conf_body, seq=seq),
        out_shape=jax.ShapeDtypeStruct((batch * seq, D_CONF), _BF16),
        grid=(batch,),
        in_specs=[pl.BlockSpec((seq, D_CONF), lambda i: (i, 0)), _const_spec((CONF_K, D_CONF)),
                  _const_spec((1, D_CONF)), _const_spec((1, D_CONF)), _const_spec((1, D_CONF))],
        out_specs=pl.BlockSpec((seq, D_CONF), lambda i: (i, 0)),
        scratch_shapes=[pltpu.VMEM((seq + 5 * SUBLANES, D_CONF), _F32)],
        compiler_params=_params("arbitrary"),
        name="conf_prompt",
    )(u, w, b, g, lb)


def _conf_step_body(buf_ref, u_ref, w_ref, b_ref, g_ref, lb_ref, c_ref, nbuf_ref):
    u = u_ref[...]
    conv = u * w_ref[CONF_K - 1:CONF_K, :]
    for k in range(CONF_K - 1):
        conv = conv + buf_ref[k] * w_ref[k:k + 1, :]
    c_ref[...] = _layernorm_silu(conv + b_ref[...], g_ref[...], lb_ref[...]).astype(_BF16)
    for k in range(CONF_K - 2):
        nbuf_ref[k] = buf_ref[k + 1]
    nbuf_ref[CONF_K - 2] = u


def _conf_step(buf, layer, u, w, b, g, lb):
    depth, _, nb, _ = buf.shape
    assert depth == 1
    bb = 16
    state = pl.BlockSpec((None, CONF_K - 1, bb, D_CONF), lambda i: (layer, 0, i, 0))
    return pl.pallas_call(
        _conf_step_body,
        out_shape=(jax.ShapeDtypeStruct((nb, D_CONF), _BF16), jax.ShapeDtypeStruct(buf.shape, buf.dtype)),
        grid=(nb // bb,),
        in_specs=[state, pl.BlockSpec((bb, D_CONF), lambda i: (i, 0)), _const_spec((CONF_K, D_CONF)),
                  _const_spec((1, D_CONF)), _const_spec((1, D_CONF)), _const_spec((1, D_CONF))],
        out_specs=(pl.BlockSpec((bb, D_CONF), lambda i: (i, 0)), state),
        compiler_params=_params("arbitrary"),
        name="conf_step",
    )(buf, u, w, b, g, lb)


def _lru_gates(lx, wg_ref, bg_ref, lam_ref, head):
    sl = slice(head * HEAD, (head + 1) * HEAD)
    lxh = lx[:, sl]
    z = _dot(lxh.astype(_BF16), wg_ref[head]) + bg_ref[head:head + 1, :]
    r = jax.nn.sigmoid(z[:, :HEAD])
    i = jax.nn.sigmoid(z[:, HEAD:])
    log_a = -LRU_C * r * jax.nn.softplus(-lam_ref[:, sl])
    a = jnp.exp(log_a)
    gain2 = -jnp.tanh(log_a) * (a * a + 1.0)
    gain = jnp.where(gain2 > 0.0, gain2 * lax.rsqrt(gain2), 0.0)
    return a, gain * i * lxh


def _lru_body(lx_ref, lg_ref, cw_ref, cb_ref, wg_ref, bg_ref, lam_ref, r_ref, hl_ref,
              halo_ref, h_ref, a_ref, u_ref, hs_ref):
    j = pl.program_id(1)
    tc = LRU_TC

    @pl.when(j == 0)
    def _():
        halo_ref[...] = jnp.zeros_like(halo_ref)
        h_ref[...] = jnp.zeros_like(h_ref)

    cur = lx_ref[...]
    ext = jnp.concatenate([halo_ref[...], cur], axis=0)
    lx = cb_ref[...] + cw_ref[LRU_K - 1:LRU_K, :] * cur
    for back in range(1, LRU_K):
        lx = lx + cw_ref[LRU_K - 1 - back:LRU_K - back, :] * ext[SUBLANES - back:SUBLANES - back + tc, :]
    halo_ref[...] = cur[tc - SUBLANES:, :]

    for head in range(N_HEADS):
        a, u = _lru_gates(lx, wg_ref, bg_ref, lam_ref, head)
        a_ref[head * LRU_PITCH:head * LRU_PITCH + tc, :] = a
        u_ref[head * LRU_PITCH:head * LRU_PITCH + tc, :] = u

    def step(t, h):
        rows = pl.ds(t, N_HEADS, stride=LRU_PITCH)
        h = a_ref[rows, :] * h + u_ref[rows, :]
        hs_ref[rows, :] = h
        return h

    h_ref[...] = lax.fori_loop(0, tc, step, h_ref[...], unroll=16)

    for head in range(N_HEADS):
        sl = slice(head * HEAD, (head + 1) * HEAD)
        hs = hs_ref[head * LRU_PITCH:head * LRU_PITCH + tc, :]
        r_ref[:, sl] = (hs * lg_ref[:, sl]).astype(_BF16)

    @pl.when(j == pl.num_programs(1) - 1)
    def _():
        hl_ref[0] = h_ref[...]


def _lru_prompt(lx, lg, cw, cb, wg, bg, lam, batch, seq):
    nchunk = seq // LRU_TC
    tok = pl.BlockSpec((LRU_TC, D_LRU), lambda b, j: (b * nchunk + j, 0))
    pitch_rows = N_HEADS * LRU_PITCH
    return pl.pallas_call(
        _lru_body,
        out_shape=(jax.ShapeDtypeStruct((batch * seq, D_LRU), _BF16),
                   jax.ShapeDtypeStruct((batch, N_HEADS, HEAD), _F32)),
        grid=(batch, nchunk),
        in_specs=[tok, tok, _const_spec((LRU_K, D_LRU)), _const_spec((1, D_LRU)),
                  _const_spec((N_HEADS, HEAD, 2 * HEAD)), _const_spec((N_HEADS, 2 * HEAD)),
                  _const_spec((1, D_LRU))],
        out_specs=(tok, pl.BlockSpec((1, N_HEADS, HEAD), lambda b, j: (b, 0, 0))),
        scratch_shapes=[pltpu.VMEM((SUBLANES, D_LRU), _F32), pltpu.VMEM((N_HEADS, HEAD), _F32),
                        pltpu.VMEM((pitch_rows, HEAD), _F32), pltpu.VMEM((pitch_rows, HEAD), _F32),
                        pltpu.VMEM((pitch_rows, HEAD), _F32)],
        compiler_params=_params("arbitrary", "arbitrary"),
        name="lru_prompt",
    )(lx, lg, cw, cb, wg, bg, lam)


def _lru_step_body(buf_ref, lx_ref, h0_ref, lg_ref, cw_ref, cb_ref, wg_ref, bg_ref, lam_ref,
                   r_ref, hn_ref, nbuf_ref):
    cur = lx_ref[...]
    lx = cb_ref[...] + cw_ref[LRU_K - 1:LRU_K, :] * cur
    for k in range(LRU_K - 1):
        lx = lx + cw_ref[k:k + 1, :] * buf_ref[:, k, :]
    for k in range(LRU_K - 2):
        nbuf_ref[:, k, :] = buf_ref[:, k + 1, :]
    nbuf_ref[:, LRU_K - 2, :] = cur
    for head in range(N_HEADS):
        sl = slice(head * HEAD, (head + 1) * HEAD)
        a, u = _lru_gates(lx, wg_ref, bg_ref, lam_ref, head)
        h = a * h0_ref[:, sl] + u
        hn_ref[:, sl] = h
        r_ref[:, sl] = (h * lg_ref[:, sl]).astype(_BF16)


def _lru_step(buf, layer, lx, h0, lg, cw, cb, wg, bg, lam):
    depth, nb = buf.shape[:2]
    assert depth == 1
    full = _const_spec((nb, D_LRU))
    state = pl.BlockSpec((None, nb, LRU_K - 1, D_LRU), lambda i: (layer, 0, 0, 0))
    tok = pl.BlockSpec((nb, D_LRU), lambda i: (0, 0))
    return pl.pallas_call(
        _lru_step_body,
        out_shape=(jax.ShapeDtypeStruct((nb, D_LRU), _BF16), jax.ShapeDtypeStruct((nb, D_LRU), _F32),
                   jax.ShapeDtypeStruct(buf.shape, buf.dtype)),
        grid=(1,),
        in_specs=[state, full, full, full, _const_spec((LRU_K, D_LRU)), _const_spec((1, D_LRU)),
                  _const_spec((N_HEADS, HEAD, 2 * HEAD)), _const_spec((N_HEADS, 2 * HEAD)),
                  _const_spec((1, D_LRU))],
        out_specs=(tok, tok, state),
        compiler_params=_params("arbitrary"),
        name="lru_step",
    )(buf, lx, h0, lg, cw, cb, wg, bg, lam)


R_E1, R_E2, R_W1, R_W2, R_RANK1, R_RANK2 = range(6)


def _route_tile(logits, seen):
    m = logits.shape[0]
    lane = lax.broadcasted_iota(jnp.int32, (m, LANES), 1).astype(_F32)
    neg = jnp.float32(-jnp.inf)

    def first_max(vals):
        vmax = jnp.max(vals, axis=1, keepdims=True)
        return vmax, jnp.min(jnp.where(vals == vmax, lane, float(LANES)), axis=1, keepdims=True)

    in_groups = lane < N_GROUPS
    g_max, g_idx = first_max(jnp.where(in_groups, logits, neg))
    g_den = jnp.sum(jnp.where(in_groups, jnp.exp(logits - g_max), 0.0), axis=1, keepdims=True)
    lo = N_GROUPS + EPG * g_idx
    e_logits = jnp.where((lane >= lo) & (lane < lo + EPG), logits, neg)
    v1, i1 = first_max(e_logits)
    v2, i2 = first_max(jnp.where(lane == i1, neg, e_logits))
    t = jnp.exp(v2 - v1)
    w1 = 1.0 / (g_den * (1.0 + t))
    w2 = w1 * t
    e1 = i1 - N_GROUPS
    e2 = i2 - N_GROUPS

    is1 = lane == e1
    is2 = lane == e2
    onehot = jnp.where(is1 | is2, 1.0, 0.0)
    earlier = (lax.broadcasted_iota(jnp.int32, (m, m), 1) < lax.broadcasted_iota(jnp.int32, (m, m), 0))
    before = seen + _dot(jnp.where(earlier, 1.0, 0.0).astype(_BF16), onehot.astype(_BF16))
    rank1 = jnp.sum(jnp.where(is1, before, 0.0), axis=1, keepdims=True)
    rank2 = jnp.sum(jnp.where(is2, before, 0.0), axis=1, keepdims=True)

    rec = jnp.zeros((m, LANES), _F32)
    for pos, val in ((R_E1, e1), (R_E2, e2), (R_W1, w1), (R_W2, w2), (R_RANK1, rank1), (R_RANK2, rank2)):
        rec = jnp.where(lane == pos, val, rec)
    return rec, seen + jnp.sum(onehot, axis=0, keepdims=True)


def _outproj_body(cp_ref, rp_ref, xp_ref, cs_ref, rs_ref, xs_ref, w_hbm, g_ref, wrt_ref, brt_ref,
                  h1_ref, xn_ref, rec_ref, cnt_ref, seen_ref, w_b, stage, wsem, *, n_ptiles, nb):
    i = pl.program_id(0)

    @pl.when(i == 0)
    def _():
        seen_ref[...] = jnp.zeros_like(seen_ref)
        _cast_weight_resident(w_hbm, w_b, stage, wsem)

    def mix(c, r, x):
        h1 = x + _dot(c, w_b[:D_CONF, :]) + _dot(r, w_b[D_CONF:, :])
        xn = _rms(h1, g_ref[...])
        logits = _dot(xn.astype(_BF16), wrt_ref[...]) + brt_ref[...]
        rec, seen = _route_tile(logits, seen_ref[...])
        seen_ref[...] = seen
        return h1, xn, rec

    @pl.when(i < n_ptiles)
    def _():
        h1_ref[...], xn_ref[...], rec_ref[...] = mix(cp_ref[...], rp_ref[...], xp_ref[...])

    @pl.when(i == n_ptiles)
    def _():
        for ref, val in zip((h1_ref, xn_ref, rec_ref), mix(cs_ref[...], rs_ref[...], xs_ref[...])):
            ref[:nb, :] = val
            ref[nb:, :] = jnp.zeros((ref.shape[0] - nb, ref.shape[1]), _F32)

    cnt_ref[...] = jnp.broadcast_to(seen_ref[...], cnt_ref.shape)


def _outproj(cp, rp, xp, cs, rs, xs, w, g, wrt, brt, tm):
    n_ptiles = xp.shape[0] // tm
    nb = xs.shape[0]
    rows = (n_ptiles + 1) * tm
    last = n_ptiles - 1
    tok = lambda width: pl.BlockSpec((tm, width), lambda i: (jnp.minimum(i, last), 0))
    out_tok = lambda width: pl.BlockSpec((tm, width), lambda i: (i, 0))
    return pl.pallas_call(
        functools.partial(_outproj_body, n_ptiles=n_ptiles, nb=nb),
        out_shape=(jax.ShapeDtypeStruct((rows, D_MODEL), _F32), jax.ShapeDtypeStruct((rows, D_MODEL), _F32),
                   jax.ShapeDtypeStruct((rows, LANES), _F32), jax.ShapeDtypeStruct((SUBLANES, LANES), _F32)),
        grid=(n_ptiles + 1,),
        in_specs=[tok(D_CONF), tok(D_LRU), tok(D_MODEL), _const_spec((nb, D_CONF)), _const_spec((nb, D_LRU)),
                  _const_spec((nb, D_MODEL)), pl.BlockSpec(memory_space=pl.ANY),
                  _const_spec((1, D_MODEL)), _const_spec((D_MODEL, LANES)), _const_spec((1, LANES))],
        out_specs=(out_tok(D_MODEL), out_tok(D_MODEL), out_tok(LANES),
                   pl.BlockSpec((SUBLANES, LANES), lambda i: (0, 0))),
        scratch_shapes=[pltpu.VMEM((1, LANES), _F32)] + _weight_scratch(D_CONF + D_LRU, D_MODEL),
        compiler_params=_params("arbitrary"),
        name="outproj",
    )(cp, rp, xp, cs, rs, xs, w, g, wrt, brt)


def _dispatch_body(slot_ref, lo_ref, hi_ref, x_hbm, xs_hbm, xbuf, zeros, xsem, sem, zsem, *, tm, n_full, tail):
    i = pl.program_id(0)

    def zero_copies(n, wait):
        lo, hi = lo_ref[n], hi_ref[n]
        head_end = jnp.minimum(((lo + SUBLANES - 1) // SUBLANES) * SUBLANES, hi)

        def go(copy):
            copy.wait() if wait else copy.start()

        def single(s, carry):
            go(pltpu.make_async_copy(zeros.at[pl.ds(0, 1), :], xs_hbm.at[pl.ds(s, 1), :], zsem))
            return carry
        lax.fori_loop(lo, head_end, single, 0)

        size = hi - head_end
        n_bufs = size // TM_E

        def whole(c, carry):
            start = pl.multiple_of(head_end + c * TM_E, SUBLANES)
            go(pltpu.make_async_copy(zeros, xs_hbm.at[pl.ds(start, TM_E), :], zsem))
            return carry
        lax.fori_loop(0, n_bufs, whole, 0)
        start = pl.multiple_of(head_end + n_bufs * TM_E, SUBLANES)
        rest = pl.multiple_of(size - n_bufs * TM_E, SUBLANES)

        @pl.when(rest > 0)
        def _():
            go(pltpu.make_async_copy(zeros.at[pl.ds(0, rest), :], xs_hbm.at[pl.ds(start, rest), :], zsem))

    cur = lax.rem(i, 2)
    x_ref = xbuf.at[lax.rem(i, 3)]

    def fetch(tile):
        buf = lax.rem(tile, 3)
        return pltpu.make_async_copy(x_hbm.at[pl.ds(pl.multiple_of(tile * tm, tm), tm), :], xbuf.at[buf], xsem.at[buf])

    def start_scatter(rows):
        for r in range(rows):
            for k in range(2):
                s = slot_ref[(i * tm + r) * 2 + k]
                pltpu.make_async_copy(x_ref.at[pl.ds(r, 1), :], xs_hbm.at[pl.ds(s, 1), :],
                                      sem.at[cur]).start(priority=k)

    def wait_scatter(rows, parity):
        for k in range(2):
            pltpu.make_async_copy(x_ref.at[pl.ds(0, rows), :], xs_hbm.at[pl.ds(0, rows), :], sem.at[parity]).wait()

    @pl.when(i == 0)
    def _():
        fetch(0).start()
        zeros[...] = jnp.zeros_like(zeros)
        for wait in (False, True):
            lax.fori_loop(0, N_EXPERTS + 1, lambda n, c: (zero_copies(n, wait), c)[1], 0)

    @pl.when(i + 1 < pl.num_programs(0))
    def _():
        fetch(i + 1).start()

    fetch(i).wait()

    @pl.when(i < n_full)
    def _():
        start_scatter(tm)

    @pl.when(i == n_full)
    def _():
        start_scatter(tail)

    @pl.when(i > 0)
    def _():
        wait_scatter(tm, 1 - cur)

    @pl.when(i == pl.num_programs(0) - 1)
    def _():
        wait_scatter(tail if tail else tm, cur)


def _dispatch(slots, pad_lo, pad_hi, xn, n_all, n_rows, tm):
    n_full, tail = divmod(n_all, tm)
    grid_spec = pltpu.PrefetchScalarGridSpec(
        num_scalar_prefetch=3,
        grid=(n_full + (1 if tail else 0),),
        in_specs=[pl.BlockSpec(memory_space=pl.ANY)],
        out_specs=pl.BlockSpec(memory_space=pl.ANY),
        scratch_shapes=[pltpu.VMEM((3, tm, D_MODEL), _F32), pltpu.VMEM((TM_E, D_MODEL), _F32),
                        pltpu.SemaphoreType.DMA((3,)), pltpu.SemaphoreType.DMA((2,)), pltpu.SemaphoreType.DMA(())],
    )
    return pl.pallas_call(
        functools.partial(_dispatch_body, tm=tm, n_full=n_full, tail=tail),
        out_shape=jax.ShapeDtypeStruct((n_rows, D_MODEL), _F32),
        grid_spec=grid_spec,
        compiler_params=_params("arbitrary"),
        name="dispatch",
    )(slots, pad_lo, pad_hi, xn)


def _moe_body(te_ref, nx_ref, nu_ref, x_ref, wg_hbm, wu_hbm, wd_hbm, y_ref,
              wg_f, wu_f, wd_f, side_ref, wsem):
    j = pl.program_id(0)
    n_used = nu_ref[0]

    def weight_copies(e, side):
        return (pltpu.make_async_copy(wg_hbm.at[e], wg_f.at[side], wsem.at[0, side]),
                pltpu.make_async_copy(wu_hbm.at[e], wu_f.at[side], wsem.at[1, side]),
                pltpu.make_async_copy(wd_hbm.at[e], wd_f.at[side], wsem.at[2, side]))

    @pl.when(j == 0)
    def _():
        side_ref[0] = 1
        for copy in weight_copies(te_ref[0], 0):
            copy.start()

    @pl.when(j < n_used)
    def _():
        @pl.when((j == 0) | (te_ref[j] != te_ref[jnp.maximum(j - 1, 0)]))
        def _():
            side = 1 - side_ref[0]
            side_ref[0] = side
            for copy in weight_copies(te_ref[j], side):
                copy.wait()

            @pl.when(nx_ref[j] >= 0)
            def _():
                for copy in weight_copies(nx_ref[j], 1 - side):
                    copy.start(priority=1)

        side = side_ref[0]
        x = x_ref[...].astype(_BF16)
        g = _dot(x, wg_f[side].astype(_BF16))
        u = _dot(x, wu_f[side].astype(_BF16))
        hid = g * jax.nn.sigmoid(g) * u
        _store_token_major(y_ref, _dot(hid.astype(_BF16), wd_f[side].astype(_BF16)))

    @pl.when(j >= n_used)
    def _():
        y_ref[...] = jnp.zeros_like(y_ref)


def _moe(tile_expert, next_expert, n_used, xs, w_gate, w_up, w_down):
    n_tiles = tile_expert.shape[0]
    any_space = pl.BlockSpec(memory_space=pl.ANY)
    up_shape, down_shape = (D_MODEL, D_EXPERT), (D_EXPERT, D_MODEL)
    grid_spec = pltpu.PrefetchScalarGridSpec(
        num_scalar_prefetch=3,
        grid=(n_tiles,),
        in_specs=[pl.BlockSpec((TM_E, D_MODEL), lambda j, te, nx, nu: (jnp.minimum(j, nu[0] - 1), 0)),
                  any_space, any_space, any_space],
        out_specs=pl.BlockSpec((TM_E * TOK_ROWS, LANES), lambda j, te, nx, nu: (j, 0)),
        scratch_shapes=[pltpu.VMEM((2,) + up_shape, _F32), pltpu.VMEM((2,) + up_shape, _F32),
                        pltpu.VMEM((2,) + down_shape, _F32), pltpu.SMEM((1,), jnp.int32),
                        pltpu.SemaphoreType.DMA((3, 2))],
    )
    return pl.pallas_call(
        _moe_body,
        out_shape=jax.ShapeDtypeStruct((n_tiles * TM_E * TOK_ROWS, LANES), _F32),
        grid_spec=grid_spec,
        compiler_params=_params("arbitrary"),
        name="moe",
    )(tile_expert, next_expert, n_used, xs, w_gate, w_up, w_down)


def _ple_body(slot_ref, h1_ref, rec_ref, p_ref, g_ref, wpg_ref, bpg_ref, wple_ref, fg_ref, ys_hbm, o_ref,
              ybuf, sem, *, tm):
    i = pl.program_id(0)
    cur = lax.rem(i, 2)

    def start_gather(tile, buf):
        for r in range(tm):
            for k in range(2):
                row0 = pl.multiple_of(slot_ref[(tile * tm + r) * 2 + k] * TOK_ROWS, TOK_ROWS)
                pltpu.make_async_copy(ys_hbm.at[pl.ds(row0, TOK_ROWS), :],
                                      ybuf.at[buf, k, pl.ds(r * TOK_PITCH, TOK_ROWS), :], sem.at[buf]).start()

    def wait_gather(buf):
        rows = pl.ds(0, tm * TOK_ROWS)
        for k in range(2):
            pltpu.make_async_copy(ys_hbm.at[rows, :], ybuf.at[buf, k, rows, :], sem.at[buf]).wait()

    def finish():
        rec = rec_ref[...]
        y1, y2 = (_load_token_major(lambda rows, k=k: ybuf[cur, k, rows, :], tm, TOK_PITCH) for k in range(2))
        h2 = h1_ref[...] + rec[:, R_W1:R_W1 + 1] * y1 + rec[:, R_W2:R_W2 + 1] * y2
        hn = _rms(h2, g_ref[...]).astype(_BF16)
        gate = jax.nn.sigmoid(_dot(hn, wpg_ref[...]) + bpg_ref[...])
        pe = _dot(p_ref[...].astype(_BF16), wple_ref[...])
        o_ref[...] = _rms(h2 + gate * pe, fg_ref[...])

    @pl.when(i == 0)
    def _():
        start_gather(0, 0)

    wait_gather(cur)

    @pl.when(i + 1 < pl.num_programs(0))
    def _():
        finish()
        start_gather(i + 1, 1 - cur)

    @pl.when(i + 1 == pl.num_programs(0))
    def _():
        finish()


def _ple(h1, ys, slots, rec, p, g, wpg, bpg, wple, fg, tm, row0):
    n = p.shape[0]
    off = row0 // tm
    tok = lambda width: pl.BlockSpec((tm, width), lambda i, s: (i + off, 0))
    const = lambda shape: pl.BlockSpec(shape, lambda i, s: (0,) * len(shape), pipeline_mode=pl.Buffered(1))
    grid_spec = pltpu.PrefetchScalarGridSpec(
        num_scalar_prefetch=1,
        grid=(n // tm,),
        in_specs=[tok(D_MODEL), tok(LANES), pl.BlockSpec((tm, D_PLE), lambda i, s: (i, 0)),
                  const((1, D_MODEL)), const((D_MODEL, D_MODEL)), const((1, D_MODEL)),
                  const((D_PLE, D_MODEL)), const((1, D_MODEL)), pl.BlockSpec(memory_space=pl.ANY)],
        out_specs=pl.BlockSpec((tm, D_MODEL), lambda i, s: (i, 0)),
        scratch_shapes=[pltpu.VMEM((2, 2, tm * TOK_PITCH, LANES), _F32), pltpu.SemaphoreType.DMA((2,))],
    )
    return pl.pallas_call(
        functools.partial(_ple_body, tm=tm),
        out_shape=jax.ShapeDtypeStruct((n, D_MODEL), _F32),
        grid_spec=grid_spec,
        compiler_params=_params("arbitrary"),
        name="ple",
    )(slots, h1, rec, p, g, wpg, bpg, wple, fg, ys)


def _schedule(rec, counts_row, n_all, n_tiles):
    ids = jnp.arange(N_EXPERTS, dtype=jnp.int32)
    expert = rec[:n_all, R_E1:R_E2 + 1].astype(jnp.int32)
    rank = rec[:n_all, R_RANK1:R_RANK2 + 1].astype(jnp.int32)
    counts = counts_row[0, :N_EXPERTS].astype(jnp.int32)
    tiles_per = (counts + TM_E - 1) // TM_E
    tiles_end = jnp.cumsum(tiles_per)
    n_used = tiles_end[-1]
    first_row = (tiles_end - tiles_per) * TM_E
    slot = jnp.sum(jnp.where(expert[..., None] == ids, first_row, 0), axis=-1) + rank
    pad_lo = jnp.concatenate([first_row + counts, (n_used * TM_E).reshape(1)]).astype(jnp.int32)
    pad_hi = jnp.concatenate([tiles_end * TM_E, jnp.full((1,), n_tiles * TM_E)]).astype(jnp.int32)
    tile_ids = jnp.arange(n_tiles, dtype=jnp.int32)
    te = jnp.minimum(jnp.sum((tiles_end[None, :] <= tile_ids[:, None]).astype(jnp.int32), axis=1), N_EXPERTS - 1)
    last_expert = jnp.sum(jnp.where(tile_ids == n_used - 1, te, 0))
    te = jnp.where(tile_ids < n_used, te, last_expert).astype(jnp.int32)
    later = (ids[None, :] > te[:, None]) & (counts[None, :] > 0)
    nxt = jnp.min(jnp.where(later, ids[None, :], N_EXPERTS), axis=1)
    nxt = jnp.where(nxt < N_EXPERTS, nxt, -1).astype(jnp.int32)
    return te, nxt, n_used.astype(jnp.int32).reshape(1), slot.reshape(-1), pad_lo, pad_hi


def kernel(x_prompt, x_sample, state_conf_conv, state_lru_conv, state_lru_h, p_prompt, p_sample, norm1_g, w_in, b_in, conf_dw_w, conf_dw_b, conf_ln_g, conf_ln_b, lru_conv_w, lru_conv_b, lru_wa, lru_ba, lru_wx, lru_bx, lru_lambda, w_out, norm2_g, w_grp, b_grp, w_rt, b_rt, w_gate, w_up, w_down, ple_norm_g, w_ple, w_pg, b_pg, final_g):
    batch, seq, _ = x_prompt.shape
    nb = x_sample.shape[0]
    n_p = batch * seq
    n_all = n_p + nb
    assert w_in.shape[0] == 1
    layer = 0

    row = lambda v: v.reshape(1, -1)
    w_router = jnp.concatenate(
        [w_grp[layer], w_rt[layer], jnp.zeros((D_MODEL, LANES - N_GROUPS - N_EXPERTS), _F32)], axis=1).astype(_BF16)
    b_router = jnp.concatenate([b_grp[layer], b_rt[layer], jnp.zeros((LANES - N_GROUPS - N_EXPERTS,), _F32)])
    w_gates = jnp.concatenate([lru_wa[layer], lru_wx[layer]], axis=-1).astype(_BF16)
    b_gates = jnp.concatenate([lru_ba[layer], lru_bx[layer]], axis=-1)
    w_pg_b = w_pg[layer].astype(_BF16)
    w_ple_b = w_ple[layer].astype(_BF16)
    conf_args = (conf_dw_w[layer], row(conf_dw_b[layer]), row(conf_ln_g[layer]), row(conf_ln_b[layer]))
    lru_args = (lru_conv_w[layer], row(lru_conv_b[layer]), w_gates, b_gates, row(lru_lambda[layer]))
    out_args = (w_out[layer], row(norm2_g[layer]), w_router, row(b_router))
    ple_args = (row(ple_norm_g[layer]), w_pg_b, row(b_pg[layer]), w_ple_b, row(final_g))

    xp = x_prompt.reshape(n_p, D_MODEL)
    xs = x_sample.reshape(nb, D_MODEL)
    u_p, lx_p, lg_p, u_s, lx_s, lg_s = _inproj(xp, xs, row(norm1_g[layer]), w_in[layer], row(b_in[layer]), TM)

    c_p = _conf_prompt(u_p, *conf_args, batch, seq)
    r_p, hl_p = _lru_prompt(lx_p, lg_p, *lru_args, batch, seq)

    c_s, conf_buf_new = _conf_step(jnp.swapaxes(state_conf_conv, 1, 2), layer, u_s, *conf_args)
    r_s, hn_s, lru_buf_new = _lru_step(state_lru_conv, layer, lx_s, state_lru_h[layer], lg_s, *lru_args)
    h1, xn, rec, counts = _outproj(c_p, r_p, xp, c_s, r_s, xs, *out_args, TM)

    n_tiles = (2 * n_all + TM_E - 1) // TM_E + N_EXPERTS
    tile_expert, next_expert, n_used, slots, pad_lo, pad_hi = _schedule(rec, counts, n_all, n_tiles)
    xs = _dispatch(slots, pad_lo, pad_hi, xn, n_all, n_tiles * TM_E, TM)
    ys = _moe(tile_expert, next_expert, n_used, xs, w_gate[layer], w_up[layer], w_down[layer])

    y_p = _ple(h1, ys, slots[:2 * n_p], rec, p_prompt[layer].reshape(n_p, D_PLE), *ple_args, TM_PLE, 0)
    y_s = _ple(h1, ys, slots[2 * n_p:], rec, p_sample[layer].reshape(nb, D_PLE), *ple_args, nb, n_p)

    u_p3 = u_p.reshape(batch, seq, D_CONF)
    lx_p3 = lx_p.reshape(batch, seq, D_LRU)
    return (
        y_p.reshape(batch, seq, D_MODEL),
        y_s.reshape(nb, 1, D_MODEL),
        u_p3[None, :, seq - (CONF_K - 1):],
        jnp.swapaxes(conf_buf_new, 1, 2),
        lx_p3[None, :, seq - (LRU_K - 1):],
        lru_buf_new,
        hl_p.reshape(1, batch, D_LRU),
        hn_s[None],
    )
```

```python
import functools

import jax
import jax.numpy as jnp
from jax import lax
from jax.experimental import pallas as pl
from jax.experimental.pallas import tpu as pltpu

D_MODEL = 2048
D_CONF = 1024
D_LRU = 1024
N_HEADS = 8
HEAD = 128
CONF_K = 31
LRU_K = 4
LRU_C = 8.0
N_GROUPS = 4
EPG = 8
N_EXPERTS = 32
D_EXPERT = 512
D_PLE = 256
EPS = 1e-6

LANES = 128
SUBLANES = 8
VMEM_LIMIT = 56 * 1024 * 1024

TM = 512
TM_E = 256
CONF_TC = 256
LRU_TC = 512
LRU_PITCH = LRU_TC + SUBLANES
TM_PLE = 256
TOK_ROWS = D_MODEL // LANES
TOK_PITCH = TOK_ROWS + SUBLANES

_BF16 = jnp.bfloat16
_F32 = jnp.float32


def _params(*sem):
    return pltpu.CompilerParams(dimension_semantics=sem, vmem_limit_bytes=VMEM_LIMIT)


def _const_spec(shape):
    nd = len(shape)
    return pl.BlockSpec(shape, lambda *_: (0,) * nd, pipeline_mode=pl.Buffered(1))


def _store_token_major(ref, val, pitch=TOK_ROWS):
    for tb in range(val.shape[0] // SUBLANES):
        for s in range(TOK_ROWS):
            ref[pl.ds(tb * SUBLANES * pitch + s, SUBLANES, stride=pitch), :] = (
                val[tb * SUBLANES:(tb + 1) * SUBLANES, s * LANES:(s + 1) * LANES])


def _load_token_major(load, m, pitch):
    slabs = [jnp.concatenate([load(pl.ds(tb * SUBLANES * pitch + s, SUBLANES, stride=pitch))
                              for tb in range(m // SUBLANES)], axis=0) for s in range(TOK_ROWS)]
    return jnp.concatenate(slabs, axis=1)


def _rms(x, g):
    return x * lax.rsqrt(jnp.mean(x * x, axis=-1, keepdims=True) + EPS) * g


def _dot(a, b):
    return jnp.dot(a, b, preferred_element_type=_F32)


W_CHUNK = 256


def _cast_weight_resident(w_hbm, w_b, stage, sem):
    n_chunks = w_b.shape[1] // W_CHUNK

    def copy(c):
        return pltpu.make_async_copy(w_hbm.at[:, pl.ds(c * W_CHUNK, W_CHUNK)], stage.at[c % 2], sem.at[c % 2])

    copy(0).start()
    for c in range(n_chunks):
        if c + 1 < n_chunks:
            copy(c + 1).start()
        copy(c).wait()
        w_b[:, c * W_CHUNK:(c + 1) * W_CHUNK] = stage[c % 2].astype(_BF16)


def _weight_scratch(rows, cols):
    return [pltpu.VMEM((rows, cols), _BF16), pltpu.VMEM((2, rows, W_CHUNK), _F32), pltpu.SemaphoreType.DMA((2,))]


def _inproj_body(xp_ref, xs_ref, g_ref, w_hbm, b_ref, up_ref, lxp_ref, lgp_ref, us_ref, lxs_ref, lgs_ref,
                 w_b, stage, wsem, *, n_ptiles):
    i = pl.program_id(0)

    @pl.when(i == 0)
    def _():
        _cast_weight_resident(w_hbm, w_b, stage, wsem)

    def project(x_ref, u_ref, lx_ref, lg_ref):
        xn = _rms(x_ref[...], g_ref[...]).astype(_BF16)

        def proj(k):
            sl = slice(k * D_CONF, (k + 1) * D_CONF)
            return _dot(xn, w_b[:, sl]) + b_ref[:, sl]

        u_ref[...] = proj(0) * jax.nn.sigmoid(proj(1))
        lx_ref[...] = proj(2)
        lg_ref[...] = jax.nn.gelu(proj(3))

    @pl.when(i < n_ptiles)
    def _():
        project(xp_ref, up_ref, lxp_ref, lgp_ref)

    @pl.when(i == n_ptiles)
    def _():
        project(xs_ref, us_ref, lxs_ref, lgs_ref)


def _inproj(xp, xs, g, w, b, tm):
    n_ptiles = xp.shape[0] // tm
    nb = xs.shape[0]
    last = n_ptiles - 1
    tok = lambda width: pl.BlockSpec((tm, width), lambda i: (jnp.minimum(i, last), 0))
    sample = pl.BlockSpec((nb, D_CONF), lambda i: (0, 0))
    p_out = jax.ShapeDtypeStruct((xp.shape[0], D_CONF), _F32)
    s_out = jax.ShapeDtypeStruct((nb, D_CONF), _F32)
    return pl.pallas_call(
        functools.partial(_inproj_body, n_ptiles=n_ptiles),
        out_shape=(p_out, p_out, p_out, s_out, s_out, s_out),
        grid=(n_ptiles + 1,),
        in_specs=[tok(D_MODEL), _const_spec((nb, D_MODEL)), _const_spec((1, D_MODEL)),
                  pl.BlockSpec(memory_space=pl.ANY), _const_spec((1, 4 * D_CONF))],
        out_specs=(tok(D_CONF), tok(D_CONF), tok(D_CONF), sample, sample, sample),
        scratch_shapes=_weight_scratch(D_MODEL, 4 * D_CONF),
        compiler_params=_params("arbitrary"),
        name="inproj",
    )(xp, xs, g, w, b)


def _layernorm_silu(conv, g, b):
    mu = jnp.mean(conv, axis=-1, keepdims=True)
    cen = conv - mu
    var = jnp.mean(cen * cen, axis=-1, keepdims=True)
    y = cen * lax.rsqrt(var + EPS) * g + b
    return y * jax.nn.sigmoid(y)


def _conf_body(u_ref, w_ref, b_ref, g_ref, lb_ref, c_ref, s_ref, *, seq):
    head = 4 * SUBLANES
    s_ref[0:head, :] = jnp.zeros((head, D_CONF), _F32)
    s_ref[head:head + seq, :] = u_ref[...]
    s_ref[head + seq:, :] = jnp.zeros((SUBLANES, D_CONF), _F32)
    tc = CONF_TC

    def chunk(ci, carry):
        t0 = pl.multiple_of(ci * tc, tc)
        win = s_ref[pl.ds(t0, tc + 5 * SUBLANES), :]
        out = None
        for r in range(SUBLANES):
            part = None
            for q in range(5):
                m = SUBLANES * q + r
                if 2 <= m <= CONF_K + 1:
                    term = win[SUBLANES * q:SUBLANES * q + tc + SUBLANES, :] * w_ref[m - 2:m - 1, :]
                    part = term if part is None else part + term
            shifted = part[r:r + tc, :]
            out = shifted if out is None else out + shifted
        c = _layernorm_silu(out + b_ref[...], g_ref[...], lb_ref[...])
        c_ref[pl.ds(t0, tc), :] = c.astype(_BF16)
        return carry

    lax.fori_loop(0, seq // tc, chunk, 0)


def _conf_prompt(u, w, b, g, lb, batch, seq):
    return pl.pallas_call(
        functools.partial(_conf_body, seq=seq),
        out_shape=jax.ShapeDtypeStruct((batch * seq, D_CONF), _BF16),
        grid=(batch,),
        in_specs=[pl.BlockSpec((seq, D_CONF), lambda i: (i, 0)), _const_spec((CONF_K, D_CONF)),
                  _const_spec((1, D_CONF)), _const_spec((1, D_CONF)), _const_spec((1, D_CONF))],
        out_specs=pl.BlockSpec((seq, D_CONF), lambda i: (i, 0)),
        scratch_shapes=[pltpu.VMEM((seq + 5 * SUBLANES, D_CONF), _F32)],
        compiler_params=_params("arbitrary"),
        name="conf_prompt",
    )(u, w, b, g, lb)


def _conf_step_body(buf_ref, u_ref, w_ref, b_ref, g_ref, lb_ref, c_ref, nbuf_ref):
    u = u_ref[...]
    conv = u * w_ref[CONF_K - 1:CONF_K, :]
    for k in range(CONF_K - 1):
        conv = conv + buf_ref[k] * w_ref[k:k + 1, :]
    c_ref[...] = _layernorm_silu(conv + b_ref[...], g_ref[...], lb_ref[...]).astype(_BF16)
    for k in range(CONF_K - 2):
        nbuf_ref[k] = buf_ref[k + 1]
    nbuf_ref[CONF_K - 2] = u


def _conf_step(buf, layer, u, w, b, g, lb):
    depth, _, nb, _ = buf.shape
    assert depth == 1
    bb = 16
    state = pl.BlockSpec((None, CONF_K - 1, bb, D_CONF), lambda i: (layer, 0, i, 0))
    return pl.pallas_call(
        _conf_step_body,
        out_shape=(jax.ShapeDtypeStruct((nb, D_CONF), _BF16), jax.ShapeDtypeStruct(buf.shape, buf.dtype)),
        grid=(nb // bb,),
        in_specs=[state, pl.BlockSpec((bb, D_CONF), lambda i: (i, 0)), _const_spec((CONF_K, D_CONF)),
                  _const_spec((1, D_CONF)), _const_spec((1, D_CONF)), _const_spec((1, D_CONF))],
        out_specs=(pl.BlockSpec((bb, D_CONF), lambda i: (i, 0)), state),
        compiler_params=_params("arbitrary"),
        name="conf_step",
    )(buf, u, w, b, g, lb)


def _lru_gates(lx, wg_ref, bg_ref, lam_ref, head):
    sl = slice(head * HEAD, (head + 1) * HEAD)
    lxh = lx[:, sl]
    z = _dot(lxh.astype(_BF16), wg_ref[head]) + bg_ref[head:head + 1, :]
    r = jax.nn.sigmoid(z[:, :HEAD])
    i = jax.nn.sigmoid(z[:, HEAD:])
    log_a = -LRU_C * r * jax.nn.softplus(-lam_ref[:, sl])
    a = jnp.exp(log_a)
    gain2 = -jnp.tanh(log_a) * (a * a + 1.0)
    gain = jnp.where(gain2 > 0.0, gain2 * lax.rsqrt(gain2), 0.0)
    return a, gain * i * lxh


def _lru_body(lx_ref, lg_ref, cw_ref, cb_ref, wg_ref, bg_ref, lam_ref, r_ref, hl_ref,
              halo_ref, h_ref, a_ref, u_ref, hs_ref):
    j = pl.program_id(1)
    tc = LRU_TC

    @pl.when(j == 0)
    def _():
        halo_ref[...] = jnp.zeros_like(halo_ref)
        h_ref[...] = jnp.zeros_like(h_ref)

    cur = lx_ref[...]
    ext = jnp.concatenate([halo_ref[...], cur], axis=0)
    lx = cb_ref[...] + cw_ref[LRU_K - 1:LRU_K, :] * cur
    for back in range(1, LRU_K):
        lx = lx + cw_ref[LRU_K - 1 - back:LRU_K - back, :] * ext[SUBLANES - back:SUBLANES - back + tc, :]
    halo_ref[...] = cur[tc - SUBLANES:, :]

    for head in range(N_HEADS):
        a, u = _lru_gates(lx, wg_ref, bg_ref, lam_ref, head)
        a_ref[head * LRU_PITCH:head * LRU_PITCH + tc, :] = a
        u_ref[head * LRU_PITCH:head * LRU_PITCH + tc, :] = u

    def step(t, h):
        rows = pl.ds(t, N_HEADS, stride=LRU_PITCH)
        h = a_ref[rows, :] * h + u_ref[rows, :]
        hs_ref[rows, :] = h
        return h

    h_ref[...] = lax.fori_loop(0, tc, step, h_ref[...], unroll=16)

    for head in range(N_HEADS):
        sl = slice(head * HEAD, (head + 1) * HEAD)
        hs = hs_ref[head * LRU_PITCH:head * LRU_PITCH + tc, :]
        r_ref[:, sl] = (hs * lg_ref[:, sl]).astype(_BF16)

    @pl.when(j == pl.num_programs(1) - 1)
    def _():
        hl_ref[0] = h_ref[...]


def _lru_prompt(lx, lg, cw, cb, wg, bg, lam, batch, seq):
    nchunk = seq // LRU_TC
    tok = pl.BlockSpec((LRU_TC, D_LRU), lambda b, j: (b * nchunk + j, 0))
    pitch_rows = N_HEADS * LRU_PITCH
    return pl.pallas_call(
        _lru_body,
        out_shape=(jax.ShapeDtypeStruct((batch * seq, D_LRU), _BF16),
                   jax.ShapeDtypeStruct((batch, N_HEADS, HEAD), _F32)),
        grid=(batch, nchunk),
        in_specs=[tok, tok, _const_spec((LRU_K, D_LRU)), _const_spec((1, D_LRU)),
                  _const_spec((N_HEADS, HEAD, 2 * HEAD)), _const_spec((N_HEADS, 2 * HEAD)),
                  _const_spec((1, D_LRU))],
        out_specs=(tok, pl.BlockSpec((1, N_HEADS, HEAD), lambda b, j: (b, 0, 0))),
        scratch_shapes=[pltpu.VMEM((SUBLANES, D_LRU), _F32), pltpu.VMEM((N_HEADS, HEAD), _F32),
                        pltpu.VMEM((pitch_rows, HEAD), _F32), pltpu.VMEM((pitch_rows, HEAD), _F32),
                        pltpu.VMEM((pitch_rows, HEAD), _F32)],
        compiler_params=_params("arbitrary", "arbitrary"),
        name="lru_prompt",
    )(lx, lg, cw, cb, wg, bg, lam)


def _lru_step_body(buf_ref, lx_ref, h0_ref, lg_ref, cw_ref, cb_ref, wg_ref, bg_ref, lam_ref,
                   r_ref, hn_ref, nbuf_ref):
    cur = lx_ref[...]
    lx = cb_ref[...] + cw_ref[LRU_K - 1:LRU_K, :] * cur
    for k in range(LRU_K - 1):
        lx = lx + cw_ref[k:k + 1, :] * buf_ref[:, k, :]
    for k in range(LRU_K - 2):
        nbuf_ref[:, k, :] = buf_ref[:, k + 1, :]
    nbuf_ref[:, LRU_K - 2, :] = cur
    for head in range(N_HEADS):
        sl = slice(head * HEAD, (head + 1) * HEAD)
        a, u = _lru_gates(lx, wg_ref, bg_ref, lam_ref, head)
        h = a * h0_ref[:, sl] + u
        hn_ref[:, sl] = h
        r_ref[:, sl] = (h * lg_ref[:, sl]).astype(_BF16)


def _lru_step(buf, layer, lx, h0, lg, cw, cb, wg, bg, lam):
    depth, nb = buf.shape[:2]
    assert depth == 1
    full = _const_spec((nb, D_LRU))
    state = pl.BlockSpec((None, nb, LRU_K - 1, D_LRU), lambda i: (layer, 0, 0, 0))
    tok = pl.BlockSpec((nb, D_LRU), lambda i: (0, 0))
    return pl.pallas_call(
        _lru_step_body,
        out_shape=(jax.ShapeDtypeStruct((nb, D_LRU), _BF16), jax.ShapeDtypeStruct((nb, D_LRU), _F32),
                   jax.ShapeDtypeStruct(buf.shape, buf.dtype)),
        grid=(1,),
        in_specs=[state, full, full, full, _const_spec((LRU_K, D_LRU)), _const_spec((1, D_LRU)),
                  _const_spec((N_HEADS, HEAD, 2 * HEAD)), _const_spec((N_HEADS, 2 * HEAD)),
                  _const_spec((1, D_LRU))],
        out_specs=(tok, tok, state),
        compiler_params=_params("arbitrary"),
        name="lru_step",
    )(buf, lx, h0, lg, cw, cb, wg, bg, lam)


R_E1, R_E2, R_W1, R_W2, R_RANK1, R_RANK2 = range(6)


def _route_tile(logits, seen):
    m = logits.shape[0]
    lane = lax.broadcasted_iota(jnp.int32, (m, LANES), 1).astype(_F32)
    neg = jnp.float32(-jnp.inf)

    def first_max(vals):
        vmax = jnp.max(vals, axis=1, keepdims=True)
        return vmax, jnp.min(jnp.where(vals == vmax, lane, float(LANES)), axis=1, keepdims=True)

    in_groups = lane < N_GROUPS
    g_max, g_idx = first_max(jnp.where(in_groups, logits, neg))
    g_den = jnp.sum(jnp.where(in_groups, jnp.exp(logits - g_max), 0.0), axis=1, keepdims=True)
    lo = N_GROUPS + EPG * g_idx
    e_logits = jnp.where((lane >= lo) & (lane < lo + EPG), logits, neg)
    v1, i1 = first_max(e_logits)
    v2, i2 = first_max(jnp.where(lane == i1, neg, e_logits))
    t = jnp.exp(v2 - v1)
    w1 = 1.0 / (g_den * (1.0 + t))
    w2 = w1 * t
    e1 = i1 - N_GROUPS
    e2 = i2 - N_GROUPS

    is1 = lane == e1
    is2 = lane == e2
    onehot = jnp.where(is1 | is2, 1.0, 0.0)
    earlier = (lax.broadcasted_iota(jnp.int32, (m, m), 1) < lax.broadcasted_iota(jnp.int32, (m, m), 0))
    before = seen + _dot(jnp.where(earlier, 1.0, 0.0).astype(_BF16), onehot.astype(_BF16))
    rank1 = jnp.sum(jnp.where(is1, before, 0.0), axis=1, keepdims=True)
    rank2 = jnp.sum(jnp.where(is2, before, 0.0), axis=1, keepdims=True)

    rec = jnp.zeros((m, LANES), _F32)
    for pos, val in ((R_E1, e1), (R_E2, e2), (R_W1, w1), (R_W2, w2), (R_RANK1, rank1), (R_RANK2, rank2)):
        rec = jnp.where(lane == pos, val, rec)
    return rec, seen + jnp.sum(onehot, axis=0, keepdims=True)


def _outproj_body(cp_ref, rp_ref, xp_ref, cs_ref, rs_ref, xs_ref, w_hbm, g_ref, wrt_ref, brt_ref,
                  h1_ref, xn_ref, rec_ref, cnt_ref, seen_ref, w_b, stage, wsem, *, n_ptiles, nb):
    i = pl.program_id(0)

    @pl.when(i == 0)
    def _():
        seen_ref[...] = jnp.zeros_like(seen_ref)
        _cast_weight_resident(w_hbm, w_b, stage, wsem)

    def mix(c, r, x):
        h1 = x + _dot(c, w_b[:D_CONF, :]) + _dot(r, w_b[D_CONF:, :])
        xn = _rms(h1, g_ref[...])
        logits = _dot(xn.astype(_BF16), wrt_ref[...]) + brt_ref[...]
        rec, seen = _route_tile(logits, seen_ref[...])
        seen_ref[...] = seen
        return h1, xn, rec

    @pl.when(i < n_ptiles)
    def _():
        h1_ref[...], xn_ref[...], rec_ref[...] = mix(cp_ref[...], rp_ref[...], xp_ref[...])

    @pl.when(i == n_ptiles)
    def _():
        for ref, val in zip((h1_ref, xn_ref, rec_ref), mix(cs_ref[...], rs_ref[...], xs_ref[...])):
            ref[:nb, :] = val
            ref[nb:, :] = jnp.zeros((ref.shape[0] - nb, ref.shape[1]), _F32)

    cnt_ref[...] = jnp.broadcast_to(seen_ref[...], cnt_ref.shape)


def _outproj(cp, rp, xp, cs, rs, xs, w, g, wrt, brt, tm):
    n_ptiles = xp.shape[0] // tm
    nb = xs.shape[0]
    rows = (n_ptiles + 1) * tm
    last = n_ptiles - 1
    tok = lambda width: pl.BlockSpec((tm, width), lambda i: (jnp.minimum(i, last), 0))
    out_tok = lambda width: pl.BlockSpec((tm, width), lambda i: (i, 0))
    return pl.pallas_call(
        functools.partial(_outproj_body, n_ptiles=n_ptiles, nb=nb),
        out_shape=(jax.ShapeDtypeStruct((rows, D_MODEL), _F32), jax.ShapeDtypeStruct((rows, D_MODEL), _F32),
                   jax.ShapeDtypeStruct((rows, LANES), _F32), jax.ShapeDtypeStruct((SUBLANES, LANES), _F32)),
        grid=(n_ptiles + 1,),
        in_specs=[tok(D_CONF), tok(D_LRU), tok(D_MODEL), _const_spec((nb, D_CONF)), _const_spec((nb, D_LRU)),
                  _const_spec((nb, D_MODEL)), pl.BlockSpec(memory_space=pl.ANY),
                  _const_spec((1, D_MODEL)), _const_spec((D_MODEL, LANES)), _const_spec((1, LANES))],
        out_specs=(out_tok(D_MODEL), out_tok(D_MODEL), out_tok(LANES),
                   pl.BlockSpec((SUBLANES, LANES), lambda i: (0, 0))),
        scratch_shapes=[pltpu.VMEM((1, LANES), _F32)] + _weight_scratch(D_CONF + D_LRU, D_MODEL),
        compiler_params=_params("arbitrary"),
        name="outproj",
    )(cp, rp, xp, cs, rs, xs, w, g, wrt, brt)


def _dispatch_body(slot_ref, lo_ref, hi_ref, x_hbm, xs_hbm, xbuf, zeros, xsem, sem, zsem, *, tm, n_full, tail):
    i = pl.program_id(0)

    def zero_copies(n, wait):
        lo, hi = lo_ref[n], hi_ref[n]
        head_end = jnp.minimum(((lo + SUBLANES - 1) // SUBLANES) * SUBLANES, hi)

        def go(copy):
            copy.wait() if wait else copy.start()

        def single(s, carry):
            go(pltpu.make_async_copy(zeros.at[pl.ds(0, 1), :], xs_hbm.at[pl.ds(s, 1), :], zsem))
            return carry
        lax.fori_loop(lo, head_end, single, 0)

        size = hi - head_end
        n_bufs = size // TM_E

        def whole(c, carry):
            start = pl.multiple_of(head_end + c * TM_E, SUBLANES)
            go(pltpu.make_async_copy(zeros, xs_hbm.at[pl.ds(start, TM_E), :], zsem))
            return carry
        lax.fori_loop(0, n_bufs, whole, 0)
        start = pl.multiple_of(head_end + n_bufs * TM_E, SUBLANES)
        rest = pl.multiple_of(size - n_bufs * TM_E, SUBLANES)

        @pl.when(rest > 0)
        def _():
            go(pltpu.make_async_copy(zeros.at[pl.ds(0, rest), :], xs_hbm.at[pl.ds(start, rest), :], zsem))

    cur = lax.rem(i, 2)
    x_ref = xbuf.at[lax.rem(i, 3)]

    def fetch(tile):
        buf = lax.rem(tile, 3)
        return pltpu.make_async_copy(x_hbm.at[pl.ds(pl.multiple_of(tile * tm, tm), tm), :], xbuf.at[buf], xsem.at[buf])

    def start_scatter(rows):
        for r in range(rows):
            for k in range(2):
                s = slot_ref[(i * tm + r) * 2 + k]
                pltpu.make_async_copy(x_ref.at[pl.ds(r, 1), :], xs_hbm.at[pl.ds(s, 1), :],
                                      sem.at[cur]).start(priority=k)

    def wait_scatter(rows, parity):
        for k in range(2):
            pltpu.make_async_copy(x_ref.at[pl.ds(0, rows), :], xs_hbm.at[pl.ds(0, rows), :], sem.at[parity]).wait()

    @pl.when(i == 0)
    def _():
        fetch(0).start()
        zeros[...] = jnp.zeros_like(zeros)
        for wait in (False, True):
            lax.fori_loop(0, N_EXPERTS + 1, lambda n, c: (zero_copies(n, wait), c)[1], 0)

    @pl.when(i + 1 < pl.num_programs(0))
    def _():
        fetch(i + 1).start()

    fetch(i).wait()

    @pl.when(i < n_full)
    def _():
        start_scatter(tm)

    @pl.when(i == n_full)
    def _():
        start_scatter(tail)

    @pl.when(i > 0)
    def _():
        wait_scatter(tm, 1 - cur)

    @pl.when(i == pl.num_programs(0) - 1)
    def _():
        wait_scatter(tail if tail else tm, cur)


def _dispatch(slots, pad_lo, pad_hi, xn, n_all, n_rows, tm):
    n_full, tail = divmod(n_all, tm)
    grid_spec = pltpu.PrefetchScalarGridSpec(
        num_scalar_prefetch=3,
        grid=(n_full + (1 if tail else 0),),
        in_specs=[pl.BlockSpec(memory_space=pl.ANY)],
        out_specs=pl.BlockSpec(memory_space=pl.ANY),
        scratch_shapes=[pltpu.VMEM((3, tm, D_MODEL), _F32), pltpu.VMEM((TM_E, D_MODEL), _F32),
                        pltpu.SemaphoreType.DMA((3,)), pltpu.SemaphoreType.DMA((2,)), pltpu.SemaphoreType.DMA(())],
    )
    return pl.pallas_call(
        functools.partial(_dispatch_body, tm=tm, n_full=n_full, tail=tail),
        out_shape=jax.ShapeDtypeStruct((n_rows, D_MODEL), _F32),
        grid_spec=grid_spec,
        compiler_params=_params("arbitrary"),
        name="dispatch",
    )(slots, pad_lo, pad_hi, xn)


def _moe_body(te_ref, nx_ref, nu_ref, xs_hbm, wg_hbm, wu_hbm, wd_hbm, y_ref,
              xbuf, wg_f, wu_f, wd_f, side_ref, xsem, wsem):
    j = pl.program_id(0)
    n_used = nu_ref[0]

    def fetch(tile):
        buf = lax.rem(tile, 3)
        rows = pl.ds(pl.multiple_of(tile * TM_E, TM_E), TM_E)
        return pltpu.make_async_copy(xs_hbm.at[rows, :], xbuf.at[buf], xsem.at[buf])

    @pl.when(j == 0)
    def _():
        fetch(0).start()

        @pl.when(n_used > 1)
        def _():
            fetch(1).start()

    @pl.when(j + 2 < n_used)
    def _():
        fetch(j + 2).start()

    def weight_copies(e, side):
        return (pltpu.make_async_copy(wg_hbm.at[e], wg_f.at[side], wsem.at[0, side]),
                pltpu.make_async_copy(wu_hbm.at[e], wu_f.at[side], wsem.at[1, side]),
                pltpu.make_async_copy(wd_hbm.at[e], wd_f.at[side], wsem.at[2, side]))

    @pl.when(j == 0)
    def _():
        side_ref[0] = 1
        for copy in weight_copies(te_ref[0], 0):
            copy.start()

    @pl.when(j < n_used)
    def _():
        @pl.when((j == 0) | (te_ref[j] != te_ref[jnp.maximum(j - 1, 0)]))
        def _():
            side = 1 - side_ref[0]
            side_ref[0] = side
            for copy in weight_copies(te_ref[j], side):
                copy.wait()

            @pl.when(nx_ref[j] >= 0)
            def _():
                for copy in weight_copies(nx_ref[j], 1 - side):
                    copy.start(priority=1)

        side = side_ref[0]
        fetch(j).wait()
        x = xbuf[lax.rem(j, 3)].astype(_BF16)
        g = _dot(x, wg_f[side].astype(_BF16))
        u = _dot(x, wu_f[side].astype(_BF16))
        hid = g * jax.nn.sigmoid(g) * u
        _store_token_major(y_ref, _dot(hid.astype(_BF16), wd_f[side].astype(_BF16)))

    @pl.when(j >= n_used)
    def _():
        y_ref[...] = jnp.zeros_like(y_ref)


def _moe(tile_expert, next_expert, n_used, xs, w_gate, w_up, w_down):
    n_tiles = tile_expert.shape[0]
    any_space = pl.BlockSpec(memory_space=pl.ANY)
    up_shape, down_shape = (D_MODEL, D_EXPERT), (D_EXPERT, D_MODEL)
    grid_spec = pltpu.PrefetchScalarGridSpec(
        num_scalar_prefetch=3,
        grid=(n_tiles,),
        in_specs=[any_space, any_space, any_space, any_space],
        out_specs=pl.BlockSpec((TM_E * TOK_ROWS, LANES), lambda j, te, nx, nu: (j, 0)),
        scratch_shapes=[pltpu.VMEM((3, TM_E, D_MODEL), _F32),
                        pltpu.VMEM((2,) + up_shape, _F32), pltpu.VMEM((2,) + up_shape, _F32),
                        pltpu.VMEM((2,) + down_shape, _F32), pltpu.SMEM((1,), jnp.int32),
                        pltpu.SemaphoreType.DMA((3,)), pltpu.SemaphoreType.DMA((3, 2))],
    )
    return pl.pallas_call(
        _moe_body,
        out_shape=jax.ShapeDtypeStruct((n_tiles * TM_E * TOK_ROWS, LANES), _F32),
        grid_spec=grid_spec,
        compiler_params=_params("arbitrary"),
        name="moe",
    )(tile_expert, next_expert, n_used, xs, w_gate, w_up, w_down)


def _ple_body(slot_ref, h1_ref, rec_ref, p_ref, g_ref, wpg_ref, bpg_ref, wple_ref, fg_ref, ys_hbm, o_ref,
              ybuf, sem, *, tm):
    i = pl.program_id(0)
    cur = lax.rem(i, 2)

    def start_gather(tile, buf):
        for r in range(tm):
            for k in range(2):
                row0 = pl.multiple_of(slot_ref[(tile * tm + r) * 2 + k] * TOK_ROWS, TOK_ROWS)
                pltpu.make_async_copy(ys_hbm.at[pl.ds(row0, TOK_ROWS), :],
                                      ybuf.at[buf, k, pl.ds(r * TOK_PITCH, TOK_ROWS), :], sem.at[buf]).start()

    def wait_gather(buf):
        rows = pl.ds(0, tm * TOK_ROWS)
        for k in range(2):
            pltpu.make_async_copy(ys_hbm.at[rows, :], ybuf.at[buf, k, rows, :], sem.at[buf]).wait()

    def finish():
        rec = rec_ref[...]
        y1, y2 = (_load_token_major(lambda rows, k=k: ybuf[cur, k, rows, :], tm, TOK_PITCH) for k in range(2))
        h2 = h1_ref[...] + rec[:, R_W1:R_W1 + 1] * y1 + rec[:, R_W2:R_W2 + 1] * y2
        hn = _rms(h2, g_ref[...]).astype(_BF16)
        gate = jax.nn.sigmoid(_dot(hn, wpg_ref[...]) + bpg_ref[...])
        pe = _dot(p_ref[...].astype(_BF16), wple_ref[...])
        o_ref[...] = _rms(h2 + gate * pe, fg_ref[...])

    @pl.when(i == 0)
    def _():
        start_gather(0, 0)

    wait_gather(cur)

    @pl.when(i + 1 < pl.num_programs(0))
    def _():
        finish()
        start_gather(i + 1, 1 - cur)

    @pl.when(i + 1 == pl.num_programs(0))
    def _():
        finish()


def _ple(h1, ys, slots, rec, p, g, wpg, bpg, wple, fg, tm, row0):
    n = p.shape[0]
    off = row0 // tm
    tok = lambda width: pl.BlockSpec((tm, width), lambda i, s: (i + off, 0))
    const = lambda shape: pl.BlockSpec(shape, lambda i, s: (0,) * len(shape), pipeline_mode=pl.Buffered(1))
    grid_spec = pltpu.PrefetchScalarGridSpec(
        num_scalar_prefetch=1,
        grid=(n // tm,),
        in_specs=[tok(D_MODEL), tok(LANES), pl.BlockSpec((tm, D_PLE), lambda i, s: (i, 0)),
                  const((1, D_MODEL)), const((D_MODEL, D_MODEL)), const((1, D_MODEL)),
                  const((D_PLE, D_MODEL)), const((1, D_MODEL)), pl.BlockSpec(memory_space=pl.ANY)],
        out_specs=pl.BlockSpec((tm, D_MODEL), lambda i, s: (i, 0)),
        scratch_shapes=[pltpu.VMEM((2, 2, tm * TOK_PITCH, LANES), _F32), pltpu.SemaphoreType.DMA((2,))],
    )
    return pl.pallas_call(
        functools.partial(_ple_body, tm=tm),
        out_shape=jax.ShapeDtypeStruct((n, D_MODEL), _F32),
        grid_spec=grid_spec,
        compiler_params=_params("arbitrary"),
        name="ple",
    )(slots, h1, rec, p, g, wpg, bpg, wple, fg, ys)


def _schedule(rec, counts_row, n_all, n_tiles):
    ids = jnp.arange(N_EXPERTS, dtype=jnp.int32)
    expert = rec[:n_all, R_E1:R_E2 + 1].astype(jnp.int32)
    rank = rec[:n_all, R_RANK1:R_RANK2 + 1].astype(jnp.int32)
    counts = counts_row[0, :N_EXPERTS].astype(jnp.int32)
    tiles_per = (counts + TM_E - 1) // TM_E
    tiles_end = jnp.cumsum(tiles_per)
    n_used = tiles_end[-1]
    first_row = (tiles_end - tiles_per) * TM_E
    slot = jnp.sum(jnp.where(expert[..., None] == ids, first_row, 0), axis=-1) + rank
    pad_lo = jnp.concatenate([first_row + counts, (n_used * TM_E).reshape(1)]).astype(jnp.int32)
    pad_hi = jnp.concatenate([tiles_end * TM_E, jnp.full((1,), n_tiles * TM_E)]).astype(jnp.int32)
    tile_ids = jnp.arange(n_tiles, dtype=jnp.int32)
    te = jnp.minimum(jnp.sum((tiles_end[None, :] <= tile_ids[:, None]).astype(jnp.int32), axis=1), N_EXPERTS - 1)
    last_expert = jnp.sum(jnp.where(tile_ids == n_used - 1, te, 0))
    te = jnp.where(tile_ids < n_used, te, last_expert).astype(jnp.int32)
    later = (ids[None, :] > te[:, None]) & (counts[None, :] > 0)
    nxt = jnp.min(jnp.where(later, ids[None, :], N_EXPERTS), axis=1)
    nxt = jnp.where(nxt < N_EXPERTS, nxt, -1).astype(jnp.int32)
    return te, nxt, n_used.astype(jnp.int32).reshape(1), slot.reshape(-1), pad_lo, pad_hi


def kernel(x_prompt, x_sample, state_conf_conv, state_lru_conv, state_lru_h, p_prompt, p_sample, norm1_g, w_in, b_in, conf_dw_w, conf_dw_b, conf_ln_g, conf_ln_b, lru_conv_w, lru_conv_b, lru_wa, lru_ba, lru_wx, lru_bx, lru_lambda, w_out, norm2_g, w_grp, b_grp, w_rt, b_rt, w_gate, w_up, w_down, ple_norm_g, w_ple, w_pg, b_pg, final_g):
    batch, seq, _ = x_prompt.shape
    nb = x_sample.shape[0]
    n_p = batch * seq
    n_all = n_p + nb
    assert w_in.shape[0] == 1
    layer = 0

    row = lambda v: v.reshape(1, -1)
    w_router = jnp.concatenate(
        [w_grp[layer], w_rt[layer], jnp.zeros((D_MODEL, LANES - N_GROUPS - N_EXPERTS), _F32)], axis=1).astype(_BF16)
    b_router = jnp.concatenate([b_grp[layer], b_rt[layer], jnp.zeros((LANES - N_GROUPS - N_EXPERTS,), _F32)])
    w_gates = jnp.concatenate([lru_wa[layer], lru_wx[layer]], axis=-1).astype(_BF16)
    b_gates = jnp.concatenate([lru_ba[layer], lru_bx[layer]], axis=-1)
    w_pg_b = w_pg[layer].astype(_BF16)
    w_ple_b = w_ple[layer].astype(_BF16)
    conf_args = (conf_dw_w[layer], row(conf_dw_b[layer]), row(conf_ln_g[layer]), row(conf_ln_b[layer]))
    lru_args = (lru_conv_w[layer], row(lru_conv_b[layer]), w_gates, b_gates, row(lru_lambda[layer]))
    out_args = (w_out[layer], row(norm2_g[layer]), w_router, row(b_router))
    ple_args = (row(ple_norm_g[layer]), w_pg_b, row(b_pg[layer]), w_ple_b, row(final_g))

    xp = x_prompt.reshape(n_p, D_MODEL)
    xs = x_sample.reshape(nb, D_MODEL)
    u_p, lx_p, lg_p, u_s, lx_s, lg_s = _inproj(xp, xs, row(norm1_g[layer]), w_in[layer], row(b_in[layer]), TM)

    c_p = _conf_prompt(u_p, *conf_args, batch, seq)
    r_p, hl_p = _lru_prompt(lx_p, lg_p, *lru_args, batch, seq)

    c_s, conf_buf_new = _conf_step(jnp.swapaxes(state_conf_conv, 1, 2), layer, u_s, *conf_args)
    r_s, hn_s, lru_buf_new = _lru_step(state_lru_conv, layer, lx_s, state_lru_h[layer], lg_s, *lru_args)
    h1, xn, rec, counts = _outproj(c_p, r_p, xp, c_s, r_s, xs, *out_args, TM)

    n_tiles = (2 * n_all + TM_E - 1) // TM_E + N_EXPERTS
    tile_expert, next_expert, n_used, slots, pad_lo, pad_hi = _schedule(rec, counts, n_all, n_tiles)
    xs = _dispatch(slots, pad_lo, pad_hi, xn, n_all, n_tiles * TM_E, TM)
    ys = _moe(tile_expert, next_expert, n_used, xs, w_gate[layer], w_up[layer], w_down[layer])

    y_p = _ple(h1, ys, slots[:2 * n_p], rec, p_prompt[layer].reshape(n_p, D_PLE), *ple_args, TM_PLE, 0)
    y_s = _ple(h1, ys, slots[2 * n_p:], rec, p_sample[layer].reshape(nb, D_PLE), *ple_args, nb, n_p)

    u_p3 = u_p.reshape(batch, seq, D_CONF)
    lx_p3 = lx_p.reshape(batch, seq, D_LRU)
    return (
        y_p.reshape(batch, seq, D_MODEL),
        y_s.reshape(nb, 1, D_MODEL),
        u_p3[None, :, seq - (CONF_K - 1):],
        jnp.swapaxes(conf_buf_new, 1, 2),
        lx_p3[None, :, seq - (LRU_K - 1):],
        lru_buf_new,
        hl_p.reshape(1, batch, D_LRU),
        hn_s[None],
    )
```

```python
import functools

import jax
import jax.numpy as jnp
from jax import lax
from jax.experimental import pallas as pl
from jax.experimental.pallas import tpu as pltpu

D_MODEL = 2048
D_CONF = 1024
D_LRU = 1024
N_HEADS = 8
HEAD = 128
CONF_K = 31
LRU_K = 4
LRU_C = 8.0
N_GROUPS = 4
EPG = 8
N_EXPERTS = 32
D_EXPERT = 512
D_PLE = 256
EPS = 1e-6

LANES = 128
SUBLANES = 8
VMEM_LIMIT = 56 * 1024 * 1024

TM = 512
TM_E = 256
CONF_TC = 256
LRU_TC = 512
LRU_PITCH = LRU_TC + SUBLANES
TM_PLE = 256
TOK_ROWS = D_MODEL // LANES
TOK_PITCH = TOK_ROWS + SUBLANES

_BF16 = jnp.bfloat16
_F32 = jnp.float32


def _params(*sem):
    return pltpu.CompilerParams(dimension_semantics=sem, vmem_limit_bytes=VMEM_LIMIT)


def _const_spec(shape):
    nd = len(shape)
    return pl.BlockSpec(shape, lambda *_: (0,) * nd, pipeline_mode=pl.Buffered(1))


def _store_token_major(ref, val, pitch=TOK_ROWS):
    for tb in range(val.shape[0] // SUBLANES):
        for s in range(TOK_ROWS):
            ref[pl.ds(tb * SUBLANES * pitch + s, SUBLANES, stride=pitch), :] = (
                val[tb * SUBLANES:(tb + 1) * SUBLANES, s * LANES:(s + 1) * LANES])


def _load_token_major(load, m, pitch):
    slabs = [jnp.concatenate([load(pl.ds(tb * SUBLANES * pitch + s, SUBLANES, stride=pitch))
                              for tb in range(m // SUBLANES)], axis=0) for s in range(TOK_ROWS)]
    return jnp.concatenate(slabs, axis=1)


def _rms(x, g):
    return x * lax.rsqrt(jnp.mean(x * x, axis=-1, keepdims=True) + EPS) * g


def _dot(a, b):
    return jnp.dot(a, b, preferred_element_type=_F32)


W_CHUNK = 256


def _cast_weight_resident(w_hbm, w_b, stage, sem):
    n_chunks = w_b.shape[1] // W_CHUNK

    def copy(c):
        return pltpu.make_async_copy(w_hbm.at[:, pl.ds(c * W_CHUNK, W_CHUNK)], stage.at[c % 2], sem.at[c % 2])

    copy(0).start()
    for c in range(n_chunks):
        if c + 1 < n_chunks:
            copy(c + 1).start()
        copy(c).wait()
        w_b[:, c * W_CHUNK:(c + 1) * W_CHUNK] = stage[c % 2].astype(_BF16)


def _weight_scratch(rows, cols):
    return [pltpu.VMEM((rows, cols), _BF16), pltpu.VMEM((2, rows, W_CHUNK), _F32), pltpu.SemaphoreType.DMA((2,))]


def _inproj_body(xp_ref, xs_ref, g_ref, w_hbm, b_ref, up_ref, lxp_ref, lgp_ref, us_ref, lxs_ref, lgs_ref,
                 w_b, stage, wsem, *, n_ptiles):
    i = pl.program_id(0)

    @pl.when(i == 0)
    def _():
        _cast_weight_resident(w_hbm, w_b, stage, wsem)

    def project(x_ref, u_ref, lx_ref, lg_ref):
        xn = _rms(x_ref[...], g_ref[...]).astype(_BF16)

        def proj(k):
            sl = slice(k * D_CONF, (k + 1) * D_CONF)
            return _dot(xn, w_b[:, sl]) + b_ref[:, sl]

        u_ref[...] = proj(0) * jax.nn.sigmoid(proj(1))
        lx_ref[...] = proj(2)
        lg_ref[...] = jax.nn.gelu(proj(3))

    @pl.when(i < n_ptiles)
    def _():
        project(xp_ref, up_ref, lxp_ref, lgp_ref)

    @pl.when(i == n_ptiles)
    def _():
        project(xs_ref, us_ref, lxs_ref, lgs_ref)


def _inproj(xp, xs, g, w, b, tm):
    n_ptiles = xp.shape[0] // tm
    nb = xs.shape[0]
    last = n_ptiles - 1
    tok = lambda width: pl.BlockSpec((tm, width), lambda i: (jnp.minimum(i, last), 0))
    sample = pl.BlockSpec((nb, D_CONF), lambda i: (0, 0))
    p_out = jax.ShapeDtypeStruct((xp.shape[0], D_CONF), _F32)
    s_out = jax.ShapeDtypeStruct((nb, D_CONF), _F32)
    return pl.pallas_call(
        functools.partial(_inproj_body, n_ptiles=n_ptiles),
        out_shape=(p_out, p_out, p_out, s_out, s_out, s_out),
        grid=(n_ptiles + 1,),
        in_specs=[tok(D_MODEL), _const_spec((nb, D_MODEL)), _const_spec((1, D_MODEL)),
                  pl.BlockSpec(memory_space=pl.ANY), _const_spec((1, 4 * D_CONF))],
        out_specs=(tok(D_CONF), tok(D_CONF), tok(D_CONF), sample, sample, sample),
        scratch_shapes=_weight_scratch(D_MODEL, 4 * D_CONF),
        compiler_params=_params("arbitrary"),
        name="inproj",
    )(xp, xs, g, w, b)


def _layernorm_silu(conv, g, b):
    mu = jnp.mean(conv, axis=-1, keepdims=True)
    cen = conv - mu
    var = jnp.mean(cen * cen, axis=-1, keepdims=True)
    y = cen * lax.rsqrt(var + EPS) * g + b
    return y * jax.nn.sigmoid(y)


def _conf_body(u_ref, w_ref, b_ref, g_ref, lb_ref, c_ref, s_ref, *, seq):
    head = 4 * SUBLANES
    s_ref[0:head, :] = jnp.zeros((head, D_CONF), _F32)
    s_ref[head:head + seq, :] = u_ref[...]
    s_ref[head + seq:, :] = jnp.zeros((SUBLANES, D_CONF), _F32)
    tc = CONF_TC

    def chunk(ci, carry):
        t0 = pl.multiple_of(ci * tc, tc)
        win = s_ref[pl.ds(t0, tc + 5 * SUBLANES), :]
        out = None
        for r in range(SUBLANES):
            part = None
            for q in range(5):
                m = SUBLANES * q + r
                if 2 <= m <= CONF_K + 1:
                    term = win[SUBLANES * q:SUBLANES * q + tc + SUBLANES, :] * w_ref[m - 2:m - 1, :]
                    part = term if part is None else part + term
            shifted = part[r:r + tc, :]
            out = shifted if out is None else out + shifted
        c = _layernorm_silu(out + b_ref[...], g_ref[...], lb_ref[...])
        c_ref[pl.ds(t0, tc), :] = c.astype(_BF16)
        return carry

    lax.fori_loop(0, seq // tc, chunk, 0)


def _conf_prompt(u, w, b, g, lb, batch, seq):
    return pl.pallas_call(
        functools.partial(_conf_body, seq=seq),
        out_shape=jax.ShapeDtypeStruct((batch * seq, D_CONF), _BF16),
        grid=(batch,),
        in_specs=[pl.BlockSpec((seq, D_CONF), lambda i: (i, 0)), _const_spec((CONF_K, D_CONF)),
                  _const_spec((1, D_CONF)), _const_spec((1, D_CONF)), _const_spec((1, D_CONF))],
        out_specs=pl.BlockSpec((seq, D_CONF), lambda i: (i, 0)),
        scratch_shapes=[pltpu.VMEM((seq + 5 * SUBLANES, D_CONF), _F32)],
        compiler_params=_params("arbitrary"),
        name="conf_prompt",
    )(u, w, b, g, lb)


def _conf_step_body(buf_ref, u_ref, w_ref, b_ref, g_ref, lb_ref, c_ref, nbuf_ref):
    u = u_ref[...]
    conv = u * w_ref[CONF_K - 1:CONF_K, :]
    for k in range(CONF_K - 1):
        conv = conv + buf_ref[k] * w_ref[k:k + 1, :]
    c_ref[...] = _layernorm_silu(conv + b_ref[...], g_ref[...], lb_ref[...]).astype(_BF16)
    for k in range(CONF_K - 2):
        nbuf_ref[k] = buf_ref[k + 1]
    nbuf_ref[CONF_K - 2] = u


def _conf_step(buf, layer, u, w, b, g, lb):
    depth, _, nb, _ = buf.shape
    assert depth == 1
    bb = 16
    state = pl.BlockSpec((None, CONF_K - 1, bb, D_CONF), lambda i: (layer, 0, i, 0))
    return pl.pallas_call(
        _conf_step_body,
        out_shape=(jax.ShapeDtypeStruct((nb, D_CONF), _BF16), jax.ShapeDtypeStruct(buf.shape, buf.dtype)),
        grid=(nb // bb,),
        in_specs=[state, pl.BlockSpec((bb, D_CONF), lambda i: (i, 0)), _const_spec((CONF_K, D_CONF)),
                  _const_spec((1, D_CONF)), _const_spec((1, D_CONF)), _const_spec((1, D_CONF))],
        out_specs=(pl.BlockSpec((bb, D_CONF), lambda i: (i, 0)), state),
        compiler_params=_params("arbitrary"),
        name="conf_step",
    )(buf, u, w, b, g, lb)


def _lru_gates(lx, wg_ref, bg_ref, lam_ref, head):
    sl = slice(head * HEAD, (head + 1) * HEAD)
    lxh = lx[:, sl]
    z = _dot(lxh.astype(_BF16), wg_ref[head]) + bg_ref[head:head + 1, :]
    r = jax.nn.sigmoid(z[:, :HEAD])
    i = jax.nn.sigmoid(z[:, HEAD:])
    log_a = -LRU_C * r * jax.nn.softplus(-lam_ref[:, sl])
    a = jnp.exp(log_a)
    gain2 = -jnp.tanh(log_a) * (a * a + 1.0)
    gain = jnp.where(gain2 > 0.0, gain2 * lax.rsqrt(gain2), 0.0)
    return a, gain * i * lxh


def _lru_body(lx_ref, lg_ref, cw_ref, cb_ref, wg_ref, bg_ref, lam_ref, r_ref, hl_ref,
              halo_ref, h_ref, a_ref, u_ref, hs_ref):
    j = pl.program_id(1)
    tc = LRU_TC

    @pl.when(j == 0)
    def _():
        halo_ref[...] = jnp.zeros_like(halo_ref)
        h_ref[...] = jnp.zeros_like(h_ref)

    cur = lx_ref[...]
    ext = jnp.concatenate([halo_ref[...], cur], axis=0)
    lx = cb_ref[...] + cw_ref[LRU_K - 1:LRU_K, :] * cur
    for back in range(1, LRU_K):
        lx = lx + cw_ref[LRU_K - 1 - back:LRU_K - back, :] * ext[SUBLANES - back:SUBLANES - back + tc, :]
    halo_ref[...] = cur[tc - SUBLANES:, :]

    for head in range(N_HEADS):
        a, u = _lru_gates(lx, wg_ref, bg_ref, lam_ref, head)
        a_ref[head * LRU_PITCH:head * LRU_PITCH + tc, :] = a
        u_ref[head * LRU_PITCH:head * LRU_PITCH + tc, :] = u

    def step(t, h):
        rows = pl.ds(t, N_HEADS, stride=LRU_PITCH)
        h = a_ref[rows, :] * h + u_ref[rows, :]
        hs_ref[rows, :] = h
        return h

    h_ref[...] = lax.fori_loop(0, tc, step, h_ref[...], unroll=16)

    for head in range(N_HEADS):
        sl = slice(head * HEAD, (head + 1) * HEAD)
        hs = hs_ref[head * LRU_PITCH:head * LRU_PITCH + tc, :]
        r_ref[:, sl] = (hs * lg_ref[:, sl]).astype(_BF16)

    @pl.when(j == pl.num_programs(1) - 1)
    def _():
        hl_ref[0] = h_ref[...]


def _lru_prompt(lx, lg, cw, cb, wg, bg, lam, batch, seq):
    nchunk = seq // LRU_TC
    tok = pl.BlockSpec((LRU_TC, D_LRU), lambda b, j: (b * nchunk + j, 0))
    pitch_rows = N_HEADS * LRU_PITCH
    return pl.pallas_call(
        _lru_body,
        out_shape=(jax.ShapeDtypeStruct((batch * seq, D_LRU), _BF16),
                   jax.ShapeDtypeStruct((batch, N_HEADS, HEAD), _F32)),
        grid=(batch, nchunk),
        in_specs=[tok, tok, _const_spec((LRU_K, D_LRU)), _const_spec((1, D_LRU)),
                  _const_spec((N_HEADS, HEAD, 2 * HEAD)), _const_spec((N_HEADS, 2 * HEAD)),
                  _const_spec((1, D_LRU))],
        out_specs=(tok, pl.BlockSpec((1, N_HEADS, HEAD), lambda b, j: (b, 0, 0))),
        scratch_shapes=[pltpu.VMEM((SUBLANES, D_LRU), _F32), pltpu.VMEM((N_HEADS, HEAD), _F32),
                        pltpu.VMEM((pitch_rows, HEAD), _F32), pltpu.VMEM((pitch_rows, HEAD), _F32),
                        pltpu.VMEM((pitch_rows, HEAD), _F32)],
        compiler_params=_params("arbitrary", "arbitrary"),
        name="lru_prompt",
    )(lx, lg, cw, cb, wg, bg, lam)


def _lru_step_body(buf_ref, lx_ref, h0_ref, lg_ref, cw_ref, cb_ref, wg_ref, bg_ref, lam_ref,
                   r_ref, hn_ref, nbuf_ref):
    cur = lx_ref[...]
    lx = cb_ref[...] + cw_ref[LRU_K - 1:LRU_K, :] * cur
    for k in range(LRU_K - 1):
        lx = lx + cw_ref[k:k + 1, :] * buf_ref[:, k, :]
    for k in range(LRU_K - 2):
        nbuf_ref[:, k, :] = buf_ref[:, k + 1, :]
    nbuf_ref[:, LRU_K - 2, :] = cur
    for head in range(N_HEADS):
        sl = slice(head * HEAD, (head + 1) * HEAD)
        a, u = _lru_gates(lx, wg_ref, bg_ref, lam_ref, head)
        h = a * h0_ref[:, sl] + u
        hn_ref[:, sl] = h
        r_ref[:, sl] = (h * lg_ref[:, sl]).astype(_BF16)


def _lru_step(buf, layer, lx, h0, lg, cw, cb, wg, bg, lam):
    depth, nb = buf.shape[:2]
    assert depth == 1
    full = _const_spec((nb, D_LRU))
    state = pl.BlockSpec((None, nb, LRU_K - 1, D_LRU), lambda i: (layer, 0, 0, 0))
    tok = pl.BlockSpec((nb, D_LRU), lambda i: (0, 0))
    return pl.pallas_call(
        _lru_step_body,
        out_shape=(jax.ShapeDtypeStruct((nb, D_LRU), _BF16), jax.ShapeDtypeStruct((nb, D_LRU), _F32),
                   jax.ShapeDtypeStruct(buf.shape, buf.dtype)),
        grid=(1,),
        in_specs=[state, full, full, full, _const_spec((LRU_K, D_LRU)), _const_spec((1, D_LRU)),
                  _const_spec((N_HEADS, HEAD, 2 * HEAD)), _const_spec((N_HEADS, 2 * HEAD)),
                  _const_spec((1, D_LRU))],
        out_specs=(tok, tok, state),
        compiler_params=_params("arbitrary"),
        name="lru_step",
    )(buf, lx, h0, lg, cw, cb, wg, bg, lam)


R_E1, R_E2, R_W1, R_W2, R_RANK1, R_RANK2 = range(6)


def _route_tile(logits, seen):
    m = logits.shape[0]
    lane = lax.broadcasted_iota(jnp.int32, (m, LANES), 1).astype(_F32)
    neg = jnp.float32(-jnp.inf)

    def first_max(vals):
        vmax = jnp.max(vals, axis=1, keepdims=True)
        return vmax, jnp.min(jnp.where(vals == vmax, lane, float(LANES)), axis=1, keepdims=True)

    in_groups = lane < N_GROUPS
    g_max, g_idx = first_max(jnp.where(in_groups, logits, neg))
    g_den = jnp.sum(jnp.where(in_groups, jnp.exp(logits - g_max), 0.0), axis=1, keepdims=True)
    lo = N_GROUPS + EPG * g_idx
    e_logits = jnp.where((lane >= lo) & (lane < lo + EPG), logits, neg)
    v1, i1 = first_max(e_logits)
    v2, i2 = first_max(jnp.where(lane == i1, neg, e_logits))
    t = jnp.exp(v2 - v1)
    w1 = 1.0 / (g_den * (1.0 + t))
    w2 = w1 * t
    e1 = i1 - N_GROUPS
    e2 = i2 - N_GROUPS

    is1 = lane == e1
    is2 = lane == e2
    onehot = jnp.where(is1 | is2, 1.0, 0.0)
    earlier = (lax.broadcasted_iota(jnp.int32, (m, m), 1) < lax.broadcasted_iota(jnp.int32, (m, m), 0))
    before = seen + _dot(jnp.where(earlier, 1.0, 0.0).astype(_BF16), onehot.astype(_BF16))
    rank1 = jnp.sum(jnp.where(is1, before, 0.0), axis=1, keepdims=True)
    rank2 = jnp.sum(jnp.where(is2, before, 0.0), axis=1, keepdims=True)

    rec = jnp.zeros((m, LANES), _F32)
    for pos, val in ((R_E1, e1), (R_E2, e2), (R_W1, w1), (R_W2, w2), (R_RANK1, rank1), (R_RANK2, rank2)):
        rec = jnp.where(lane == pos, val, rec)
    return rec, seen + jnp.sum(onehot, axis=0, keepdims=True)


def _outproj_body(cp_ref, rp_ref, xp_ref, cs_ref, rs_ref, xs_ref, w_hbm, g_ref, wrt_ref, brt_ref,
                  h1_ref, xn_ref, rec_ref, cnt_ref, seen_ref, w_b, stage, wsem, *, n_ptiles, nb):
    i = pl.program_id(0)

    @pl.when(i == 0)
    def _():
        seen_ref[...] = jnp.zeros_like(seen_ref)
        _cast_weight_resident(w_hbm, w_b, stage, wsem)

    def mix(c, r, x):
        h1 = x + _dot(c, w_b[:D_CONF, :]) + _dot(r, w_b[D_CONF:, :])
        xn = _rms(h1, g_ref[...])
        logits = _dot(xn.astype(_BF16), wrt_ref[...]) + brt_ref[...]
        rec, seen = _route_tile(logits, seen_ref[...])
        seen_ref[...] = seen
        return h1, xn, rec

    @pl.when(i < n_ptiles)
    def _():
        h1_ref[...], xn_ref[...], rec_ref[...] = mix(cp_ref[...], rp_ref[...], xp_ref[...])

    @pl.when(i == n_ptiles)
    def _():
        for ref, val in zip((h1_ref, xn_ref, rec_ref), mix(cs_ref[...], rs_ref[...], xs_ref[...])):
            ref[:nb, :] = val
            ref[nb:, :] = jnp.zeros((ref.shape[0] - nb, ref.shape[1]), _F32)

    cnt_ref[...] = jnp.broadcast_to(seen_ref[...], cnt_ref.shape)


def _outproj(cp, rp, xp, cs, rs, xs, w, g, wrt, brt, tm):
    n_ptiles = xp.shape[0] // tm
    nb = xs.shape[0]
    rows = (n_ptiles + 1) * tm
    last = n_ptiles - 1
    tok = lambda width: pl.BlockSpec((tm, width), lambda i: (jnp.minimum(i, last), 0))
    out_tok = lambda width: pl.BlockSpec((tm, width), lambda i: (i, 0))
    return pl.pallas_call(
        functools.partial(_outproj_body, n_ptiles=n_ptiles, nb=nb),
        out_shape=(jax.ShapeDtypeStruct((rows, D_MODEL), _F32), jax.ShapeDtypeStruct((rows, D_MODEL), _F32),
                   jax.ShapeDtypeStruct((rows, LANES), _F32), jax.ShapeDtypeStruct((SUBLANES, LANES), _F32)),
        grid=(n_ptiles + 1,),
        in_specs=[tok(D_CONF), tok(D_LRU), tok(D_MODEL), _const_spec((nb, D_CONF)), _const_spec((nb, D_LRU)),
                  _const_spec((nb, D_MODEL)), pl.BlockSpec(memory_space=pl.ANY),
                  _const_spec((1, D_MODEL)), _const_spec((D_MODEL, LANES)), _const_spec((1, LANES))],
        out_specs=(out_tok(D_MODEL), out_tok(D_MODEL), out_tok(LANES),
                   pl.BlockSpec((SUBLANES, LANES), lambda i: (0, 0))),
        scratch_shapes=[pltpu.VMEM((1, LANES), _F32)] + _weight_scratch(D_CONF + D_LRU, D_MODEL),
        compiler_params=_params("arbitrary"),
        name="outproj",
    )(cp, rp, xp, cs, rs, xs, w, g, wrt, brt)


def _dispatch_body(slot_ref, lo_ref, hi_ref, x_hbm, xs_hbm, xbuf, zeros, xsem, sem, zsem, *, tm, n_full, tail):
    i = pl.program_id(0)

    def zero_copies(n, wait):
        lo, hi = lo_ref[n], hi_ref[n]
        head_end = jnp.minimum(((lo + SUBLANES - 1) // SUBLANES) * SUBLANES, hi)

        def go(copy):
            copy.wait() if wait else copy.start()

        def single(s, carry):
            go(pltpu.make_async_copy(zeros.at[pl.ds(0, 1), :], xs_hbm.at[pl.ds(s, 1), :], zsem))
            return carry
        lax.fori_loop(lo, head_end, single, 0)

        size = hi - head_end
        n_bufs = size // TM_E

        def whole(c, carry):
            start = pl.multiple_of(head_end + c * TM_E, SUBLANES)
            go(pltpu.make_async_copy(zeros, xs_hbm.at[pl.ds(start, TM_E), :], zsem))
            return carry
        lax.fori_loop(0, n_bufs, whole, 0)
        start = pl.multiple_of(head_end + n_bufs * TM_E, SUBLANES)
        rest = pl.multiple_of(size - n_bufs * TM_E, SUBLANES)

        @pl.when(rest > 0)
        def _():
            go(pltpu.make_async_copy(zeros.at[pl.ds(0, rest), :], xs_hbm.at[pl.ds(start, rest), :], zsem))

    cur = lax.rem(i, 2)
    x_ref = xbuf.at[lax.rem(i, 3)]

    def fetch(tile):
        buf = lax.rem(tile, 3)
        return pltpu.make_async_copy(x_hbm.at[pl.ds(pl.multiple_of(tile * tm, tm), tm), :], xbuf.at[buf], xsem.at[buf])

    def start_scatter(rows):
        for r in range(rows):
            for k in range(2):
                s = slot_ref[(i * tm + r) * 2 + k]
                pltpu.make_async_copy(x_ref.at[pl.ds(r, 1), :], xs_hbm.at[pl.ds(s, 1), :],
                                      sem.at[cur]).start(priority=k)

    def wait_scatter(rows, parity):
        for k in range(2):
            pltpu.make_async_copy(x_ref.at[pl.ds(0, rows), :], xs_hbm.at[pl.ds(0, rows), :], sem.at[parity]).wait()

    @pl.when(i == 0)
    def _():
        fetch(0).start()
        zeros[...] = jnp.zeros_like(zeros)
        for wait in (False, True):
            lax.fori_loop(0, N_EXPERTS + 1, lambda n, c: (zero_copies(n, wait), c)[1], 0)

    @pl.when(i + 1 < pl.num_programs(0))
    def _():
        fetch(i + 1).start()

    fetch(i).wait()

    @pl.when(i < n_full)
    def _():
        start_scatter(tm)

    @pl.when(i == n_full)
    def _():
        start_scatter(tail)

    @pl.when(i > 0)
    def _():
        wait_scatter(tm, 1 - cur)

    @pl.when(i == pl.num_programs(0) - 1)
    def _():
        wait_scatter(tail if tail else tm, cur)


def _dispatch(slots, pad_lo, pad_hi, xn, n_all, n_rows, tm):
    n_full, tail = divmod(n_all, tm)
    grid_spec = pltpu.PrefetchScalarGridSpec(
        num_scalar_prefetch=3,
        grid=(n_full + (1 if tail else 0),),
        in_specs=[pl.BlockSpec(memory_space=pl.ANY)],
        out_specs=pl.BlockSpec(memory_space=pl.ANY),
        scratch_shapes=[pltpu.VMEM((3, tm, D_MODEL), _F32), pltpu.VMEM((TM_E, D_MODEL), _F32),
                        pltpu.SemaphoreType.DMA((3,)), pltpu.SemaphoreType.DMA((2,)), pltpu.SemaphoreType.DMA(())],
    )
    return pl.pallas_call(
        functools.partial(_dispatch_body, tm=tm, n_full=n_full, tail=tail),
        out_shape=jax.ShapeDtypeStruct((n_rows, D_MODEL), _F32),
        grid_spec=grid_spec,
        compiler_params=_params("arbitrary"),
        name="dispatch",
    )(slots, pad_lo, pad_hi, xn)


def _moe_body(te_ref, nx_ref, nu_ref, xs_hbm, wg_hbm, wu_hbm, wd_hbm, y_ref,
              xbuf, wg_f, wu_f, wd_f, side_ref, xsem, wsem):
    j = pl.program_id(0)
    n_used = nu_ref[0]

    def fetch(tile):
        buf = lax.rem(tile, 3)
        rows = pl.ds(pl.multiple_of(tile * TM_E, TM_E), TM_E)
        return pltpu.make_async_copy(xs_hbm.at[rows, :], xbuf.at[buf], xsem.at[buf])

    @pl.when(j == 0)
    def _():
        fetch(0).start()

        @pl.when(n_used > 1)
        def _():
            fetch(1).start()

    @pl.when(j + 2 < n_used)
    def _():
        fetch(j + 2).start(priority=1)

    def weight_copies(e, side):
        return (pltpu.make_async_copy(wg_hbm.at[e], wg_f.at[side], wsem.at[0, side]),
                pltpu.make_async_copy(wu_hbm.at[e], wu_f.at[side], wsem.at[1, side]),
                pltpu.make_async_copy(wd_hbm.at[e], wd_f.at[side], wsem.at[2, side]))

    @pl.when(j == 0)
    def _():
        side_ref[0] = 1
        for copy in weight_copies(te_ref[0], 0):
            copy.start()

    @pl.when(j < n_used)
    def _():
        @pl.when((j == 0) | (te_ref[j] != te_ref[jnp.maximum(j - 1, 0)]))
        def _():
            side = 1 - side_ref[0]
            side_ref[0] = side
            for copy in weight_copies(te_ref[j], side):
                copy.wait()

            @pl.when(nx_ref[j] >= 0)
            def _():
                for copy in weight_copies(nx_ref[j], 1 - side):
                    copy.start(priority=1)

        side = side_ref[0]
        fetch(j).wait()
        x = xbuf[lax.rem(j, 3)].astype(_BF16)
        g = _dot(x, wg_f[side].astype(_BF16))
        u = _dot(x, wu_f[side].astype(_BF16))
        hid = g * jax.nn.sigmoid(g) * u
        _store_token_major(y_ref, _dot(hid.astype(_BF16), wd_f[side].astype(_BF16)))

    @pl.when(j >= n_used)
    def _():
        y_ref[...] = jnp.zeros_like(y_ref)


def _moe(tile_expert, next_expert, n_used, xs, w_gate, w_up, w_down):
    n_tiles = tile_expert.shape[0]
    any_space = pl.BlockSpec(memory_space=pl.ANY)
    up_shape, down_shape = (D_MODEL, D_EXPERT), (D_EXPERT, D_MODEL)
    grid_spec = pltpu.PrefetchScalarGridSpec(
        num_scalar_prefetch=3,
        grid=(n_tiles,),
        in_specs=[any_space, any_space, any_space, any_space],
        out_specs=pl.BlockSpec((TM_E * TOK_ROWS, LANES), lambda j, te, nx, nu: (j, 0)),
        scratch_shapes=[pltpu.VMEM((3, TM_E, D_MODEL), _F32),
                        pltpu.VMEM((2,) + up_shape, _F32), pltpu.VMEM((2,) + up_shape, _F32),
                        pltpu.VMEM((2,) + down_shape, _F32), pltpu.SMEM((1,), jnp.int32),
                        pltpu.SemaphoreType.DMA((3,)), pltpu.SemaphoreType.DMA((3, 2))],
    )
    return pl.pallas_call(
        _moe_body,
        out_shape=jax.ShapeDtypeStruct((n_tiles * TM_E * TOK_ROWS, LANES), _F32),
        grid_spec=grid_spec,
        compiler_params=_params("arbitrary"),
        name="moe",
    )(tile_expert, next_expert, n_used, xs, w_gate, w_up, w_down)


def _ple_body(slot_ref, h1_ref, rec_ref, p_ref, g_ref, wpg_ref, bpg_ref, wple_ref, fg_ref, ys_hbm, o_ref,
              ybuf, sem, *, tm):
    i = pl.program_id(0)
    cur = lax.rem(i, 2)

    def start_gather(tile, buf):
        for r in range(tm):
            for k in range(2):
                row0 = pl.multiple_of(slot_ref[(tile * tm + r) * 2 + k] * TOK_ROWS, TOK_ROWS)
                pltpu.make_async_copy(ys_hbm.at[pl.ds(row0, TOK_ROWS), :],
                                      ybuf.at[buf, k, pl.ds(r * TOK_PITCH, TOK_ROWS), :], sem.at[buf]).start()

    def wait_gather(buf):
        rows = pl.ds(0, tm * TOK_ROWS)
        for k in range(2):
            pltpu.make_async_copy(ys_hbm.at[rows, :], ybuf.at[buf, k, rows, :], sem.at[buf]).wait()

    def finish():
        rec = rec_ref[...]
        y1, y2 = (_load_token_major(lambda rows, k=k: ybuf[cur, k, rows, :], tm, TOK_PITCH) for k in range(2))
        h2 = h1_ref[...] + rec[:, R_W1:R_W1 + 1] * y1 + rec[:, R_W2:R_W2 + 1] * y2
        hn = _rms(h2, g_ref[...]).astype(_BF16)
        gate = jax.nn.sigmoid(_dot(hn, wpg_ref[...]) + bpg_ref[...])
        pe = _dot(p_ref[...].astype(_BF16), wple_ref[...])
        o_ref[...] = _rms(h2 + gate * pe, fg_ref[...])

    @pl.when(i == 0)
    def _():
        start_gather(0, 0)

    wait_gather(cur)

    @pl.when(i + 1 < pl.num_programs(0))
    def _():
        finish()
        start_gather(i + 1, 1 - cur)

    @pl.when(i + 1 == pl.num_programs(0))
    def _():
        finish()


def _ple(h1, ys, slots, rec, p, g, wpg, bpg, wple, fg, tm, row0):
    n = p.shape[0]
    off = row0 // tm
    tok = lambda width: pl.BlockSpec((tm, width), lambda i, s: (i + off, 0))
    const = lambda shape: pl.BlockSpec(shape, lambda i, s: (0,) * len(shape), pipeline_mode=pl.Buffered(1))
    grid_spec = pltpu.PrefetchScalarGridSpec(
        num_scalar_prefetch=1,
        grid=(n // tm,),
        in_specs=[tok(D_MODEL), tok(LANES), pl.BlockSpec((tm, D_PLE), lambda i, s: (i, 0)),
                  const((1, D_MODEL)), const((D_MODEL, D_MODEL)), const((1, D_MODEL)),
                  const((D_PLE, D_MODEL)), const((1, D_MODEL)), pl.BlockSpec(memory_space=pl.ANY)],
        out_specs=pl.BlockSpec((tm, D_MODEL), lambda i, s: (i, 0)),
        scratch_shapes=[pltpu.VMEM((2, 2, tm * TOK_PITCH, LANES), _F32), pltpu.SemaphoreType.DMA((2,))],
    )
    return pl.pallas_call(
        functools.partial(_ple_body, tm=tm),
        out_shape=jax.ShapeDtypeStruct((n, D_MODEL), _F32),
        grid_spec=grid_spec,
        compiler_params=_params("arbitrary"),
        name="ple",
    )(slots, h1, rec, p, g, wpg, bpg, wple, fg, ys)


def _schedule(rec, counts_row, n_all, n_tiles):
    ids = jnp.arange(N_EXPERTS, dtype=jnp.int32)
    expert = rec[:n_all, R_E1:R_E2 + 1].astype(jnp.int32)
    rank = rec[:n_all, R_RANK1:R_RANK2 + 1].astype(jnp.int32)
    counts = counts_row[0, :N_EXPERTS].astype(jnp.int32)
    tiles_per = (counts + TM_E - 1) // TM_E
    tiles_end = jnp.cumsum(tiles_per)
    n_used = tiles_end[-1]
    first_row = (tiles_end - tiles_per) * TM_E
    slot = jnp.sum(jnp.where(expert[..., None] == ids, first_row, 0), axis=-1) + rank
    pad_lo = jnp.concatenate([first_row + counts, (n_used * TM_E).reshape(1)]).astype(jnp.int32)
    pad_hi = jnp.concatenate([tiles_end * TM_E, jnp.full((1,), n_tiles * TM_E)]).astype(jnp.int32)
    tile_ids = jnp.arange(n_tiles, dtype=jnp.int32)
    te = jnp.minimum(jnp.sum((tiles_end[None, :] <= tile_ids[:, None]).astype(jnp.int32), axis=1), N_EXPERTS - 1)
    last_expert = jnp.sum(jnp.where(tile_ids == n_used - 1, te, 0))
    te = jnp.where(tile_ids < n_used, te, last_expert).astype(jnp.int32)
    later = (ids[None, :] > te[:, None]) & (counts[None, :] > 0)
    nxt = jnp.min(jnp.where(later, ids[None, :], N_EXPERTS), axis=1)
    nxt = jnp.where(nxt < N_EXPERTS, nxt, -1).astype(jnp.int32)
    return te, nxt, n_used.astype(jnp.int32).reshape(1), slot.reshape(-1), pad_lo, pad_hi


def kernel(x_prompt, x_sample, state_conf_conv, state_lru_conv, state_lru_h, p_prompt, p_sample, norm1_g, w_in, b_in, conf_dw_w, conf_dw_b, conf_ln_g, conf_ln_b, lru_conv_w, lru_conv_b, lru_wa, lru_ba, lru_wx, lru_bx, lru_lambda, w_out, norm2_g, w_grp, b_grp, w_rt, b_rt, w_gate, w_up, w_down, ple_norm_g, w_ple, w_pg, b_pg, final_g):
    batch, seq, _ = x_prompt.shape
    nb = x_sample.shape[0]
    n_p = batch * seq
    n_all = n_p + nb
    assert w_in.shape[0] == 1
    layer = 0

    row = lambda v: v.reshape(1, -1)
    w_router = jnp.concatenate(
        [w_grp[layer], w_rt[layer], jnp.zeros((D_MODEL, LANES - N_GROUPS - N_EXPERTS), _F32)], axis=1).astype(_BF16)
    b_router = jnp.concatenate([b_grp[layer], b_rt[layer], jnp.zeros((LANES - N_GROUPS - N_EXPERTS,), _F32)])
    w_gates = jnp.concatenate([lru_wa[layer], lru_wx[layer]], axis=-1).astype(_BF16)
    b_gates = jnp.concatenate([lru_ba[layer], lru_bx[layer]], axis=-1)
    w_pg_b = w_pg[layer].astype(_BF16)
    w_ple_b = w_ple[layer].astype(_BF16)
    conf_args = (conf_dw_w[layer], row(conf_dw_b[layer]), row(conf_ln_g[layer]), row(conf_ln_b[layer]))
    lru_args = (lru_conv_w[layer], row(lru_conv_b[layer]), w_gates, b_gates, row(lru_lambda[layer]))
    out_args = (w_out[layer], row(norm2_g[layer]), w_router, row(b_router))
    ple_args = (row(ple_norm_g[layer]), w_pg_b, row(b_pg[layer]), w_ple_b, row(final_g))

    xp = x_prompt.reshape(n_p, D_MODEL)
    xs = x_sample.reshape(nb, D_MODEL)
    u_p, lx_p, lg_p, u_s, lx_s, lg_s = _inproj(xp, xs, row(norm1_g[layer]), w_in[layer], row(b_in[layer]), TM)

    c_p = _conf_prompt(u_p, *conf_args, batch, seq)
    r_p, hl_p = _lru_prompt(lx_p, lg_p, *lru_args, batch, seq)

    c_s, conf_buf_new = _conf_step(jnp.swapaxes(state_conf_conv, 1, 2), layer, u_s, *conf_args)
    r_s, hn_s, lru_buf_new = _lru_step(state_lru_conv, layer, lx_s, state_lru_h[layer], lg_s, *lru_args)
    h1, xn, rec, counts = _outproj(c_p, r_p, xp, c_s, r_s, xs, *out_args, TM)

    n_tiles = (2 * n_all + TM_E - 1) // TM_E + N_EXPERTS
    tile_expert, next_expert, n_used, slots, pad_lo, pad_hi = _schedule(rec, counts, n_all, n_tiles)
    xs = _dispatch(slots, pad_lo, pad_hi, xn, n_all, n_tiles * TM_E, TM)
    ys = _moe(tile_expert, next_expert, n_used, xs, w_gate[layer], w_up[layer], w_down[layer])

    y_p = _ple(h1, ys, slots[:2 * n_p], rec, p_prompt[layer].reshape(n_p, D_PLE), *ple_args, TM_PLE, 0)
    y_s = _ple(h1, ys, slots[2 * n_p:], rec, p_sample[layer].reshape(nb, D_PLE), *ple_args, nb, n_p)

    u_p3 = u_p.reshape(batch, seq, D_CONF)
    lx_p3 = lx_p.reshape(batch, seq, D_LRU)
    return (
        y_p.reshape(batch, seq, D_MODEL),
        y_s.reshape(nb, 1, D_MODEL),
        u_p3[None, :, seq - (CONF_K - 1):],
        jnp.swapaxes(conf_buf_new, 1, 2),
        lx_p3[None, :, seq - (LRU_K - 1):],
        lru_buf_new,
        hl_p.reshape(1, batch, D_LRU),
        hn_s[None],
    )
```
